```python
import math
import jax, jax.numpy as jnp
from jax import lax
import numpy as np

D_MODEL = 4096
BATCH = 4
SEQ = 4096
DEPTH = 4
DEC_BATCH = 8
DEC_SEQ = 2048
PAST_LEN = 128

HEAD_DIM = 128
ATTN_GROUPS = ((128, 1), (512, 4), (2048, 16))
HEADS_PER_GROUP = 4
ATT_HEADS = HEADS_PER_GROUP * len(ATTN_GROUPS)
ATT_W = ATT_HEADS * HEAD_DIM
ATT_OUT = HEADS_PER_GROUP * HEAD_DIM
ROPE_THETA = 10000.0
D_INNER = D_MODEL // 2
SSM_HEADDIM = 64
SSM_HEADS = D_INNER // SSM_HEADDIM
SSM_GROUPS = 4
D_STATE = 128
CONV_WIDTH = 3
CONV_PAD = CONV_WIDTH // 2
CONV_DIM = D_INNER + 2 * SSM_GROUPS * D_STATE
CHUNK = 128
D_FF = D_MODEL // 2
N_IN = 3 * ATT_W + D_INNER + CONV_DIM + 2 * SSM_HEADS + 2 * D_MODEL
RMS_EPS = 1e-6
NEG_INF = -1e30

kernel_name = 'hybrid_dilated_attn_ssd_macaron_encoder'


def rms_norm(x, w):
    xf = x.astype(jnp.float32)
    y = xf * lax.rsqrt(jnp.mean(xf * xf, axis=-1, keepdims=True) + RMS_EPS)
    return (y * w.astype(jnp.float32)).astype(x.dtype)


def swiglu(h, w_gate, w_up, w_down):
    return (jax.nn.silu(h @ w_gate) * (h @ w_up)) @ w_down


def rope(x, positions):
    inv_freq = ROPE_THETA ** (-jnp.arange(0, HEAD_DIM, 2, dtype=jnp.float32) / HEAD_DIM)
    ang = positions[:, None] * inv_freq[None, :]
    cos = jnp.cos(ang)[None, :, None, :]
    sin = jnp.sin(ang)[None, :, None, :]
    xf = x.astype(jnp.float32)
    x1, x2 = jnp.split(xf, 2, axis=-1)
    return jnp.concatenate([x1 * cos - x2 * sin, x2 * cos + x1 * sin], axis=-1)


def dilated_window_attention(q, k, v, window, dil):
    b, s, h, hd = q.shape
    R = window // (2 * dil)
    n = s // dil
    nb = -(-n // R)
    n_pad = nb * R
    bb = b * dil

    def to_sub(t):
        return t.reshape(b, n, dil, h, hd).transpose(0, 2, 1, 3, 4).reshape(bb, n, h, hd)

    qs, ks, vs = to_sub(q), to_sub(k), to_sub(v)
    qb = jnp.pad(qs, ((0, 0), (0, n_pad - n), (0, 0), (0, 0))).reshape(bb, nb, R, h, hd)
    kv_pad = ((0, 0), (R, n_pad - n + R), (0, 0), (0, 0))
    kp = jnp.pad(ks, kv_pad).reshape(bb, nb + 2, R, h, hd)
    vp = jnp.pad(vs, kv_pad).reshape(bb, nb + 2, R, h, hd)
    kb = jnp.concatenate([kp[:, :-2], kp[:, 1:-1], kp[:, 2:]], axis=2)
    vb = jnp.concatenate([vp[:, :-2], vp[:, 1:-1], vp[:, 2:]], axis=2)

    blk = jnp.arange(nb)[:, None] * R
    qpos = blk + jnp.arange(R)[None, :]
    kpos = blk - R + jnp.arange(3 * R)[None, :]
    rel = kpos[:, None, :] - qpos[:, :, None]
    valid = (jnp.abs(rel) <= R) & (kpos >= 0)[:, None, :] & (kpos < n)[:, None, :]

    scores = jnp.einsum('bnqhd,bnkhd->bnhqk', qb, kb).astype(jnp.float32)
    scores = jnp.where(valid[None, :, None, :, :], scores, NEG_INF)
    m = jnp.max(scores, axis=-1, keepdims=True)
    p = jnp.exp(scores - m)
    den = jnp.sum(p, axis=-1)
    o = jnp.einsum('bnhqk,bnkhd->bnqhd', p, vb.astype(jnp.float32))
    o = o / jnp.transpose(den, (0, 1, 3, 2))[..., None]
    lse = jnp.transpose(m[..., 0] + jnp.log(den), (0, 1, 3, 2))

    o = o.reshape(bb, n_pad, h, hd)[:, :n].reshape(b, dil, n, h, hd)
    o = o.transpose(0, 2, 1, 3, 4).reshape(b, s, h, hd)
    lse = lse.reshape(bb, n_pad, h)[:, :n].reshape(b, dil, n, h)
    lse = lse.transpose(0, 2, 1, 3).reshape(b, s, h)
    return o, lse


def attention_branch(q, k, v, q_norm, k_norm):
    b, s = q.shape[0], q.shape[1]
    positions = jnp.arange(s, dtype=jnp.float32)
    qn = rope(rms_norm(q, q_norm), positions) * (HEAD_DIM ** -0.5)
    kn = rope(rms_norm(k, k_norm), positions)
    vf = v.astype(jnp.float32)
    outs, lses = [], []
    for g, (window, dil) in enumerate(ATTN_GROUPS):
        sl = slice(g * HEADS_PER_GROUP, (g + 1) * HEADS_PER_GROUP)
        o, lse = dilated_window_attention(qn[:, :, sl], kn[:, :, sl], vf[:, :, sl], window, dil)
        outs.append(o)
        lses.append(lse)
    wts = jax.nn.softmax(jnp.stack(lses, axis=0), axis=0)
    o = jnp.sum(wts[..., None] * jnp.stack(outs, axis=0), axis=0)
    return o.reshape(b, s, ATT_OUT).astype(q.dtype)


def ssd_scan(x, dt, A, Bm, Cm):
    b, s = x.shape[0], x.shape[1]
    G, J, P, N = SSM_GROUPS, SSM_HEADS // SSM_GROUPS, SSM_HEADDIM, D_STATE
    nc = s // CHUNK
    f32 = jnp.float32
    xs = jnp.moveaxis(x.astype(f32).reshape(b, nc, CHUNK, G, J, P), 1, 0)
    dts = jnp.moveaxis(dt.reshape(b, nc, CHUNK, G, J), 1, 0)
    Bs = jnp.moveaxis(Bm.astype(f32).reshape(b, nc, CHUNK, G, N), 1, 0)
    Cs = jnp.moveaxis(Cm.astype(f32).reshape(b, nc, CHUNK, G, N), 1, 0)
    Ag = A.reshape(G, J)
    tril = jnp.tril(jnp.ones((CHUNK, CHUNK), dtype=bool))[None, :, :, None, None]

    def step(state, inp):
        xc, dtc, Bc, Cc = inp
        acum = jnp.cumsum(dtc * Ag, axis=1)
        diff = acum[:, :, None] - acum[:, None, :]
        decay = jnp.exp(jnp.where(tril, diff, -jnp.inf))
        cb = jnp.einsum('blgn,bsgn->blsg', Cc, Bc)
        scores = cb[..., None] * decay * dtc[:, None]
        y = jnp.einsum('blsgj,bsgjp->blgjp', scores, xc)
        y = y + jnp.einsum('blgn,bgjpn->blgjp', Cc, state) * jnp.exp(acum)[..., None]
        w_end = jnp.exp(acum[:, -1:] - acum) * dtc
        state = state * jnp.exp(acum[:, -1])[..., None, None] + jnp.einsum(
            'blgn,blgj,blgjp->bgjpn', Bc, w_end, xc)
        return state, y

    state0 = jnp.zeros((b, G, J, P, N), f32)
    _, ys = lax.scan(step, state0, (xs, dts, Bs, Cs))
    return jnp.moveaxis(ys, 0, 1).reshape(b, s, SSM_HEADS, P)


def ssd_branch(z, xbc, dt_raw, conv_w, conv_b, dt_bias, a_log, d_skip, ssm_norm):
    b, s = z.shape[0], z.shape[1]
    xbc = lax.conv_general_dilated(
        xbc, conv_w[:, None, :].astype(xbc.dtype), window_strides=(1,),
        padding=((CONV_PAD, CONV_PAD),), dimension_numbers=('NWC', 'WIO', 'NWC'),
        feature_group_count=CONV_DIM)
    xbc = jax.nn.silu(xbc + conv_b)
    xs, Bm, Cm = jnp.split(xbc, [D_INNER, D_INNER + SSM_GROUPS * D_STATE], axis=-1)
    xs = xs.reshape(b, s, SSM_HEADS, SSM_HEADDIM)
    Bm = Bm.reshape(b, s, SSM_GROUPS, D_STATE)
    Cm = Cm.reshape(b, s, SSM_GROUPS, D_STATE)
    dt = jax.nn.softplus(dt_raw.astype(jnp.float32).reshape(b, s, 2, SSM_HEADS)
                         + dt_bias.astype(jnp.float32))
    A = -jnp.exp(a_log.astype(jnp.float32))
    y_fwd = ssd_scan(xs, dt[:, :, 0], A[0], Bm, Cm)
    flip = lambda t: jnp.flip(t, axis=1)
    y_bwd = flip(ssd_scan(flip(xs), flip(dt[:, :, 1]), A[1], flip(Bm), flip(Cm)))
    y = y_fwd + y_bwd + d_skip.astype(jnp.float32)[:, None] * xs.astype(jnp.float32)
    y = y.reshape(b, s, D_INNER) * jax.nn.silu(z.astype(jnp.float32))
    return rms_norm(y, ssm_norm).astype(z.dtype)


def encoder_layer(x, ffn1_norm, ffn1_w_gate, ffn1_w_up, ffn1_w_down, mix_norm, w_in,
                  q_norm, k_norm, conv_w, conv_b, dt_bias, a_log, d_skip, ssm_norm,
                  w_attn_out, w_ssm_out, w_out, ffn2_norm, ffn2_w_gate, ffn2_w_up, ffn2_w_down):
    b, s, _ = x.shape
    x = x + 0.5 * swiglu(rms_norm(x, ffn1_norm), ffn1_w_gate, ffn1_w_up, ffn1_w_down)
    h = rms_norm(x, mix_norm)
    proj = h @ w_in
    cuts = np.cumsum([ATT_W, ATT_W, ATT_W, D_INNER, CONV_DIM, 2 * SSM_HEADS]).tolist()
    q, k, v, z, xbc, dt_raw, gate_raw = jnp.split(proj, cuts, axis=-1)
    hs = (b, s, ATT_HEADS, HEAD_DIM)
    y_att = attention_branch(q.reshape(hs), k.reshape(hs), v.reshape(hs), q_norm, k_norm) @ w_attn_out
    y_ssm = ssd_branch(z, xbc, dt_raw, conv_w, conv_b, dt_bias, a_log, d_skip, ssm_norm) @ w_ssm_out
    g_att, g_ssm = jnp.split(jax.nn.sigmoid(gate_raw), 2, axis=-1)
    x = x + (g_att * y_att + g_ssm * y_ssm) @ w_out
    x = x + 0.5 * swiglu(rms_norm(x, ffn2_norm), ffn2_w_gate, ffn2_w_up, ffn2_w_down)
    return x


def setup_inputs(seed: int = 0) -> dict:
    key = jax.random.key(seed)
    ks = jax.random.split(key, 26)
    f32 = jnp.float32

    def dense(k, shape, fan_in):
        return jax.random.normal(k, shape, f32) * (fan_in ** -0.5)

    def gain(k, shape):
        return 1.0 + 0.02 * jax.random.normal(k, shape, f32)

    L = DEPTH
    u = jax.random.uniform(ks[11], (L, 2, SSM_HEADS), f32)
    dt0 = jnp.exp(u * (math.log(0.1) - math.log(0.001)) + math.log(0.001))
    dt_bias = dt0 + jnp.log(-jnp.expm1(-dt0))
    a_log = jnp.log(jax.random.uniform(ks[12], (L, 2, SSM_HEADS), f32, 1.0, 16.0))
    return {
        'x_prompt': jax.random.normal(ks[0], (BATCH, SEQ, D_MODEL), f32),
        'x_sample': jax.random.normal(ks[1], (DEC_BATCH, DEC_SEQ, D_MODEL), f32),
        'ffn1_norm': gain(ks[2], (L, D_MODEL)),
        'ffn1_w_gate': dense(ks[3], (L, D_MODEL, D_FF), D_MODEL),
        'ffn1_w_up': dense(ks[4], (L, D_MODEL, D_FF), D_MODEL),
        'ffn1_w_down': dense(ks[5], (L, D_FF, D_MODEL), D_FF),
        'mix_norm': gain(ks[6], (L, D_MODEL)),
        'w_in': dense(ks[7], (L, D_MODEL, N_IN), D_MODEL),
        'q_norm': gain(ks[8], (L, HEAD_DIM)),
        'k_norm': gain(ks[9], (L, HEAD_DIM)),
        'conv_w': dense(ks[10], (L, CONV_WIDTH, CONV_DIM), CONV_WIDTH),
        'conv_b': 0.02 * jax.random.normal(ks[13], (L, CONV_DIM), f32),
        'dt_bias': dt_bias,
        'a_log': a_log,
        'd_skip': gain(ks[14], (L, SSM_HEADS)),
        'ssm_norm': gain(ks[15], (L, D_INNER)),
        'w_attn_out': dense(ks[16], (L, ATT_OUT, D_MODEL), ATT_OUT),
        'w_ssm_out': dense(ks[17], (L, D_INNER, D_MODEL), D_INNER),
        'w_out': dense(ks[18], (L, D_MODEL, D_MODEL), D_MODEL),
        'ffn2_norm': gain(ks[19], (L, D_MODEL)),
        'ffn2_w_gate': dense(ks[20], (L, D_MODEL, D_FF), D_MODEL),
        'ffn2_w_up': dense(ks[21], (L, D_MODEL, D_FF), D_MODEL),
        'ffn2_w_down': dense(ks[22], (L, D_FF, D_MODEL), D_FF),
    }


def reference(x_prompt, x_sample, ffn1_norm, ffn1_w_gate, ffn1_w_up, ffn1_w_down, mix_norm,
              w_in, q_norm, k_norm, conv_w, conv_b, dt_bias, a_log, d_skip, ssm_norm,
              w_attn_out, w_ssm_out, w_out, ffn2_norm, ffn2_w_gate, ffn2_w_up, ffn2_w_down):
    params = (ffn1_norm, ffn1_w_gate, ffn1_w_up, ffn1_w_down, mix_norm, w_in, q_norm, k_norm,
              conv_w, conv_b, dt_bias, a_log, d_skip, ssm_norm, w_attn_out, w_ssm_out, w_out,
              ffn2_norm, ffn2_w_gate, ffn2_w_up, ffn2_w_down)

    def trunk(x):
        for layer in range(DEPTH):
            x = encoder_layer(x, *[p[layer] for p in params])
        return x

    y_prompt = trunk(x_prompt)
    y_sample = trunk(x_sample)
    return (y_prompt, y_sample)
```

```python
import functools

import jax
import jax.numpy as jnp
from jax import lax
from jax.experimental import pallas as pl
from jax.experimental.pallas import tpu as pltpu

HEAD_DIM = 128
ATTN_GROUPS = ((128, 1), (512, 4), (2048, 16))
HEADS_PER_GROUP = 4
ATT_HEADS = HEADS_PER_GROUP * len(ATTN_GROUPS)
ATT_W = ATT_HEADS * HEAD_DIM
ATT_OUT = HEADS_PER_GROUP * HEAD_DIM
ATT_RADIUS = 64
ROPE_THETA = 10000.0
SSM_GROUPS = 4
SSM_HEADDIM = 64
CHUNK = 128
RMS_EPS = 1e-6
NEG_INF = -1e30

LANES = 128
SUBLANES = 8
V7X_VMEM_LIMIT = 56 * 1024 * 1024

F32 = jnp.float32
BF16 = jnp.bfloat16

assert all(w // (2 * d) == ATT_RADIUS for w, d in ATTN_GROUPS)


def _params(semantics):
    return pltpu.CompilerParams(dimension_semantics=semantics, vmem_limit_bytes=V7X_VMEM_LIMIT)


def _tile(n, pref):
    t = min(pref, n)
    t -= t % LANES
    while n % t:
        t -= LANES
    return t


def _rmsnorm_kernel(x_ref, w_ref, o_ref):
    x = x_ref[...]
    r = lax.rsqrt(jnp.mean(x * x, axis=-1, keepdims=True) + RMS_EPS)
    o_ref[...] = ((x * r) * w_ref[...]).astype(o_ref.dtype)


def rmsnorm_bf16(x, w, tm=256):
    m, d = x.shape
    return pl.pallas_call(
        _rmsnorm_kernel,
        grid=(m // tm,),
        in_specs=[pl.BlockSpec((tm, d), lambda i: (i, 0)), pl.BlockSpec((1, d), lambda i: (0, 0))],
        out_specs=pl.BlockSpec((tm, d), lambda i: (i, 0)),
        out_shape=jax.ShapeDtypeStruct((m, d), BF16),
        compiler_params=_params(("parallel",)),
        name="rmsnorm_bf16",
    )(x, w.reshape(1, d))


def _mm_kernel(a_ref, w_ref, o_ref):
    o_ref[...] = jnp.dot(a_ref[...], w_ref[...], preferred_element_type=F32).astype(o_ref.dtype)


def matmul(a, w, out_dtype=F32, tm=1024, tn=1024, name="matmul"):
    m, k = a.shape
    n = w.shape[1]
    tm, tn = min(tm, m), _tile(n, tn)
    return pl.pallas_call(
        _mm_kernel,
        grid=(m // tm, n // tn),
        in_specs=[pl.BlockSpec((tm, k), lambda i, j: (i, 0)), pl.BlockSpec((k, tn), lambda i, j: (0, j))],
        out_specs=pl.BlockSpec((tm, tn), lambda i, j: (i, j)),
        out_shape=jax.ShapeDtypeStruct((m, n), out_dtype),
        compiler_params=_params(("parallel", "arbitrary")),
        name=name,
    )(a, w)


def _ffn_up_kernel(h_ref, wg_ref, wu_ref, o_ref):
    h = h_ref[...]
    g = jnp.dot(h, wg_ref[...], preferred_element_type=F32)
    u = jnp.dot(h, wu_ref[...], preferred_element_type=F32)
    o_ref[...] = (g * jax.nn.sigmoid(g) * u).astype(o_ref.dtype)


def ffn_up(h, wg, wu, tm=1024, tn=512):
    m, k = h.shape
    n = wg.shape[1]
    tm, tn = min(tm, m), _tile(n, tn)
    return pl.pallas_call(
        _ffn_up_kernel,
        grid=(m // tm, n // tn),
        in_specs=[pl.BlockSpec((tm, k), lambda i, j: (i, 0)),
                  pl.BlockSpec((k, tn), lambda i, j: (0, j)),
                  pl.BlockSpec((k, tn), lambda i, j: (0, j))],
        out_specs=pl.BlockSpec((tm, tn), lambda i, j: (i, j)),
        out_shape=jax.ShapeDtypeStruct((m, n), BF16),
        compiler_params=_params(("parallel", "arbitrary")),
        name="ffn_up",
    )(h, wg, wu)


def _mm_residual_kernel(a_ref, w_ref, x_ref, o_ref, *, scale):
    y = jnp.dot(a_ref[...], w_ref[...], preferred_element_type=F32)
    o_ref[...] = x_ref[...] + scale * y


def matmul_residual(a, w, x, scale, tm=1024, tn=1024, name="matmul_residual"):
    m, k = a.shape
    n = w.shape[1]
    tm, tn = min(tm, m), _tile(n, tn)
    return pl.pallas_call(
        functools.partial(_mm_residual_kernel, scale=scale),
        grid=(m // tm, n // tn),
        in_specs=[pl.BlockSpec((tm, k), lambda i, j: (i, 0)),
                  pl.BlockSpec((k, tn), lambda i, j: (0, j)),
                  pl.BlockSpec((tm, tn), lambda i, j: (i, j))],
        out_specs=pl.BlockSpec((tm, tn), lambda i, j: (i, j)),
        out_shape=jax.ShapeDtypeStruct((m, n), F32),
        compiler_params=_params(("parallel", "arbitrary")),
        name=name,
    )(a, w, x)


def _mix_kernel(ya_ref, ys_ref, wa_ref, ws_ref, ga_ref, gs_ref, o_ref):
    pa = jnp.dot(ya_ref[...], wa_ref[...], preferred_element_type=F32)
    ps = jnp.dot(ys_ref[...], ws_ref[...], preferred_element_type=F32)
    o_ref[...] = (jax.nn.sigmoid(ga_ref[...]) * pa + jax.nn.sigmoid(gs_ref[...]) * ps).astype(o_ref.dtype)


def gated_mix(ya, ys, wa, ws, gate_raw, tm=512, tn=1024):
    m, ka = ya.shape
    ks = ys.shape[1]
    n = wa.shape[1]
    tm, tn = min(tm, m), _tile(n, tn)
    nj = n // tn
    return pl.pallas_call(
        _mix_kernel,
        grid=(m // tm, nj),
        in_specs=[pl.BlockSpec((tm, ka), lambda i, j: (i, 0)),
                  pl.BlockSpec((tm, ks), lambda i, j: (i, 0)),
                  pl.BlockSpec((ka, tn), lambda i, j: (0, j)),
                  pl.BlockSpec((ks, tn), lambda i, j: (0, j)),
                  pl.BlockSpec((tm, tn), lambda i, j: (i, j)),
                  pl.BlockSpec((tm, tn), lambda i, j: (i, j + nj))],
        out_specs=pl.BlockSpec((tm, tn), lambda i, j: (i, j)),
        out_shape=jax.ShapeDtypeStruct((m, n), BF16),
        compiler_params=_params(("parallel", "arbitrary")),
        name="gated_mix",
    )(ya, ys, wa, ws, gate_raw, gate_raw)


ATT_TQ = 128
ATT_TK = ATT_TQ + 2 * ATT_RADIUS


def _attn_group(q_ref, k_ref, v_ref, cos_ref, sin_ref, qw_ref, kw_ref,
                kd_ref, vd_ref, acc_ref, m_ref, l_ref, *, s, dil, first, last):
    n = s // dil
    nblk = n // ATT_TQ
    seg = n + 2 * ATT_RADIUS
    scale = HEAD_DIM ** -0.5

    def rows_of(c):
        r, mb = c // nblk, c % nblk
        if dil == 1:
            return r, mb, pl.ds(pl.multiple_of(c * ATT_TQ, ATT_TQ), ATT_TQ)
        return r, mb, pl.ds(r + mb * (ATT_TQ * dil), ATT_TQ, stride=dil)

    def norm_rope(x, w, rows):
        xn = x * lax.rsqrt(jnp.mean(x * x, axis=-1, keepdims=True) + RMS_EPS) * w
        return xn * cos_ref[rows, :] + pltpu.roll(xn, HEAD_DIM // 2, 1) * sin_ref[rows, :]

    zeros = jnp.zeros((ATT_RADIUS, HEAD_DIM), BF16)

    def zero_pads(r, carry):
        lo = pl.multiple_of(r * seg, ATT_RADIUS)
        hi = pl.multiple_of(r * seg + ATT_RADIUS + n, ATT_RADIUS)
        kd_ref[pl.ds(lo, ATT_RADIUS), :] = zeros
        kd_ref[pl.ds(hi, ATT_RADIUS), :] = zeros
        vd_ref[pl.ds(lo, ATT_RADIUS), :] = zeros
        vd_ref[pl.ds(hi, ATT_RADIUS), :] = zeros
        return carry

    lax.fori_loop(0, dil, zero_pads, 0)

    def prep_kv(c, carry):
        r, mb, rows = rows_of(c)
        kn = norm_rope(k_ref[rows, :], kw_ref[...], rows)
        dst = pl.ds(pl.multiple_of(r * seg + ATT_RADIUS + mb * ATT_TQ, ATT_RADIUS), ATT_TQ)
        kd_ref[dst, :] = kn.astype(BF16)
        vd_ref[dst, :] = v_ref[rows, :].astype(BF16)
        return carry

    lax.fori_loop(0, s // ATT_TQ, prep_kv, 0)

    qi = lax.broadcasted_iota(jnp.int32, (ATT_TQ, ATT_TK), 0)
    kj = lax.broadcasted_iota(jnp.int32, (ATT_TQ, ATT_TK), 1)
    band = jnp.abs(kj - ATT_RADIUS - qi) <= ATT_RADIUS

    def block(c, carry):
        r, mb, rows = rows_of(c)
        qn = (norm_rope(q_ref[rows, :], qw_ref[...], rows) * scale).astype(BF16)
        win = pl.ds(pl.multiple_of(r * seg + mb * ATT_TQ, ATT_RADIUS), ATT_TK)
        sc = lax.dot_general(qn, kd_ref[win, :], (((1,), (1,)), ((), ())), preferred_element_type=F32)
        kpos = kj + (mb * ATT_TQ - ATT_RADIUS)
        sc = jnp.where(band & (kpos >= 0) & (kpos < n), sc, NEG_INF)
        mx = jnp.max(sc, axis=-1, keepdims=True)
        p = jnp.exp(sc - mx)
        den = jnp.sum(p, axis=-1, keepdims=True)
        o = jnp.dot(p.astype(BF16), vd_ref[win, :], preferred_element_type=F32) / den
        lse = jnp.broadcast_to(mx + jnp.log(den), (ATT_TQ, HEAD_DIM))
        if first:
            acc_n, m_n, l_n = o, lse, jnp.ones_like(lse)
        else:
            m_o = m_ref[rows, :]
            m_n = jnp.maximum(m_o, lse)
            a, b = jnp.exp(m_o - m_n), jnp.exp(lse - m_n)
            acc_n = a * acc_ref[rows, :] + b * o
            l_n = a * l_ref[rows, :] + b
        if last:
            acc_ref[rows, :] = acc_n / l_n
        else:
            acc_ref[rows, :] = acc_n
            m_ref[rows, :] = m_n
            l_ref[rows, :] = l_n
        return carry

    lax.fori_loop(0, s // ATT_TQ, block, 0)


def _attn_kernel(q_ref, k_ref, v_ref, cos_ref, sin_ref, qw_ref, kw_ref, o_ref,
                 kd_ref, vd_ref, acc_ref, m_ref, l_ref, *, s):
    g = pl.program_id(2)
    ng = len(ATTN_GROUPS)
    for gi, (_, dil) in enumerate(ATTN_GROUPS):
        @pl.when(g == gi)
        def _(dil=dil, gi=gi):
            _attn_group(q_ref, k_ref, v_ref, cos_ref, sin_ref, qw_ref, kw_ref,
                        kd_ref, vd_ref, acc_ref, m_ref, l_ref,
                        s=s, dil=dil, first=gi == 0, last=gi == ng - 1)

    @pl.when(g == ng - 1)
    def _():
        o_ref[...] = acc_ref[...].astype(o_ref.dtype)


def attention(qkv, cos, sin, q_norm, k_norm, row0, b, s):
    blk0 = row0 // s
    hpg = HEADS_PER_GROUP
    max_dil = max(d for _, d in ATTN_GROUPS)
    kv_rows = s + 2 * ATT_RADIUS * max_dil

    def col(base):
        return lambda bi, j, g: (blk0 + bi, base + g * hpg + j)

    const = lambda bi, j, g: (0, 0)
    return pl.pallas_call(
        functools.partial(_attn_kernel, s=s),
        grid=(b, hpg, len(ATTN_GROUPS)),
        in_specs=[pl.BlockSpec((s, HEAD_DIM), col(0)),
                  pl.BlockSpec((s, HEAD_DIM), col(ATT_HEADS)),
                  pl.BlockSpec((s, HEAD_DIM), col(2 * ATT_HEADS)),
                  pl.BlockSpec((s, HEAD_DIM), const),
                  pl.BlockSpec((s, HEAD_DIM), const),
                  pl.BlockSpec((1, HEAD_DIM), const),
                  pl.BlockSpec((1, HEAD_DIM), const)],
        out_specs=pl.BlockSpec((s, HEAD_DIM), lambda bi, j, g: (bi, j)),
        out_shape=jax.ShapeDtypeStruct((b * s, ATT_OUT), BF16),
        scratch_shapes=[pltpu.VMEM((kv_rows, HEAD_DIM), BF16),
                        pltpu.VMEM((kv_rows, HEAD_DIM), BF16),
                        pltpu.VMEM((s, HEAD_DIM), F32),
                        pltpu.VMEM((s, HEAD_DIM), F32),
                        pltpu.VMEM((s, HEAD_DIM), F32)],
        compiler_params=_params(("parallel", "parallel", "arbitrary")),
        name=f"dilated_attention_s{s}",
    )(qkv, qkv, qkv, cos, sin, q_norm.reshape(1, HEAD_DIM), k_norm.reshape(1, HEAD_DIM))


def rope_tables(s):
    inv_freq = ROPE_THETA ** (-jnp.arange(0, HEAD_DIM, 2, dtype=F32) / HEAD_DIM)
    ang = jnp.arange(s, dtype=F32)[:, None] * inv_freq[None, :]
    cos, sin = jnp.cos(ang), jnp.sin(ang)
    return jnp.concatenate([cos, cos], axis=-1), jnp.concatenate([-sin, sin], axis=-1)


def _scan_rows(x, reverse):
    row = lax.broadcasted_iota(jnp.int32, x.shape, 0)
    sh = 1
    while sh < CHUNK:
        if reverse:
            x = x + jnp.where(row < CHUNK - sh, pltpu.roll(x, CHUNK - sh, 0), 0.0)
        else:
            x = x + jnp.where(row >= sh, pltpu.roll(x, sh, 0), 0.0)
        sh *= 2
    return x


def _ssd_kernel(*refs, nc, di, gn, reverse, finalize):
    if finalize:
        (xbc_ref, xprev_ref, xnext_ref, dt_ref, cw_ref, cb_ref, dtb_ref, alog_ref,
         yf_ref, z_ref, dskip_ref, nw_ref, o_ref, state_ref, xs_ref, y_ref) = refs
    else:
        (xbc_ref, xprev_ref, xnext_ref, dt_ref, cw_ref, cb_ref, dtb_ref, alog_ref,
         o_ref, state_ref, xs_ref) = refs
        y_ref = o_ref
    c = pl.program_id(1)
    chunk = nc - 1 - c if reverse else c
    gw = di // SSM_GROUPS
    n_state = gn // SSM_GROUPS
    half = SSM_HEADDIM

    @pl.when(c == 0)
    def _():
        state_ref[...] = jnp.zeros_like(state_ref)

    x = xbc_ref[...]
    row = lax.broadcasted_iota(jnp.int32, (CHUNK, 1), 0)
    prev_row = jnp.where(chunk > 0, xprev_ref[SUBLANES - 1:SUBLANES, :], 0.0)
    next_row = jnp.where(chunk < nc - 1, xnext_ref[0:1, :], 0.0)
    xm1 = jnp.where(row == 0, prev_row, pltpu.roll(x, 1, 0))
    xp1 = jnp.where(row == CHUNK - 1, next_row, pltpu.roll(x, CHUNK - 1, 0))
    xc = xm1 * cw_ref[0:1, :] + x * cw_ref[1:2, :] + xp1 * cw_ref[2:3, :] + cb_ref[...]
    xc = xc * jax.nn.sigmoid(xc)
    xs_ref[...] = xc[:, :di]
    bm = xc[:, di:di + gn].astype(BF16)
    cm = xc[:, di + gn:].astype(BF16)

    dtr = dt_ref[...] + dtb_ref[...]
    dt = jnp.maximum(dtr, 0.0) + jnp.log1p(jnp.exp(-jnp.abs(dtr)))
    acum = _scan_rows(dt * (-jnp.exp(alog_ref[...])), reverse)
    total = acum[0:1, :] if reverse else acum[CHUNK - 1:CHUNK, :]
    wend = jnp.exp(total - acum) * dt
    acum_t = acum.T
    dt_t = dt.T

    li = lax.broadcasted_iota(jnp.int32, (CHUNK, CHUNK), 0)
    si = lax.broadcasted_iota(jnp.int32, (CHUNK, CHUNK), 1)
    causal = (si >= li) if reverse else (li >= si)
    lane = lax.broadcasted_iota(jnp.int32, (CHUNK, LANES), 1)
    lo_half = lane < half

    def lanes_of(col_vals, h):
        return jnp.broadcast_to(col_vals[:, h:h + 1], (CHUNK, LANES))

    for g in range(SSM_GROUPS):
        b_g = bm[:, g * n_state:(g + 1) * n_state]
        c_g = cm[:, g * n_state:(g + 1) * n_state]
        cb = lax.dot_general(c_g, b_g, (((1,), (1,)), ((), ())), preferred_element_type=F32)
        st = state_ref[g]
        y_off = jnp.dot(c_g, st.astype(BF16), preferred_element_type=F32)
        wx_parts, dec_parts = [], []
        for pr in range(gw // LANES):
            col0 = g * gw + pr * LANES
            h0 = col0 // half
            xp = xs_ref[:, col0:col0 + LANES]
            sc, ecol, wcol = [], [], []
            for h in (h0, h0 + 1):
                a_col = lanes_of(acum, h)
                dec = jnp.exp(jnp.where(causal, a_col - acum_t[h:h + 1, :], -jnp.inf))
                sc.append((cb * dec * dt_t[h:h + 1, :]).astype(BF16))
                ecol.append(jnp.exp(a_col))
                wcol.append(lanes_of(wend, h))
            s2 = jnp.concatenate(sc, axis=1)
            x2 = jnp.concatenate([jnp.where(lo_half, xp, 0.0), jnp.where(lo_half, 0.0, xp)], axis=0).astype(BF16)
            y = jnp.dot(s2, x2, preferred_element_type=F32)
            y = y + y_off[:, pr * LANES:(pr + 1) * LANES] * jnp.where(lo_half, ecol[0], ecol[1])
            y_ref[:, col0:col0 + LANES] = y
            wx_parts.append((jnp.where(lo_half, wcol[0], wcol[1]) * xp).astype(BF16))
            tot0 = jnp.broadcast_to(total[:, h0:h0 + 1], (1, LANES))
            tot1 = jnp.broadcast_to(total[:, h0 + 1:h0 + 2], (1, LANES))
            dec_parts.append(jnp.exp(jnp.where(lo_half[0:1, :], tot0, tot1)))
        wx = jnp.concatenate(wx_parts, axis=1) if len(wx_parts) > 1 else wx_parts[0]
        sdec = jnp.concatenate(dec_parts, axis=1) if len(dec_parts) > 1 else dec_parts[0]
        upd = lax.dot_general(b_g, wx, (((0,), (0,)), ((), ())), preferred_element_type=F32)
        state_ref[g] = st * sdec + upd

    if finalize:
        y = yf_ref[...] + y_ref[...] + dskip_ref[...] * xs_ref[...]
        z = z_ref[...]
        y = y * (z * jax.nn.sigmoid(z))
        r = lax.rsqrt(jnp.mean(y * y, axis=-1, keepdims=True) + RMS_EPS)
        o_ref[...] = ((y * r) * nw_ref[...]).astype(o_ref.dtype)


def ssd_scan(xbc, dt_raw, conv_w, conv_b, dt_bias, a_log, row0, b, s, reverse, fin=None):
    cd = conv_w.shape[1]
    heads = dt_bias.shape[0]
    di = heads * SSM_HEADDIM
    gn = (cd - di) // 2
    nc = s // CHUNK
    cblk0 = row0 // CHUNK
    per8 = CHUNK // SUBLANES
    last8 = xbc.shape[0] // SUBLANES - 1
    direction = 1 if reverse else 0

    def chunk_of(bi, c):
        return cblk0 + bi * nc + (nc - 1 - c if reverse else c)

    def local_chunk_of(bi, c):
        return bi * nc + (nc - 1 - c if reverse else c)

    pad = lambda v: jnp.zeros((1, LANES), F32).at[0, :heads].set(v)
    const = lambda bi, c: (0, 0)
    in_specs = [pl.BlockSpec((CHUNK, cd), lambda bi, c: (chunk_of(bi, c), 0)),
                pl.BlockSpec((SUBLANES, cd), lambda bi, c: (jnp.maximum(chunk_of(bi, c) * per8 - 1, 0), 0)),
                pl.BlockSpec((SUBLANES, cd), lambda bi, c: (jnp.minimum((chunk_of(bi, c) + 1) * per8, last8), 0)),
                pl.BlockSpec((CHUNK, LANES), lambda bi, c: (chunk_of(bi, c), direction)),
                pl.BlockSpec((3, cd), const),
                pl.BlockSpec((1, cd), const),
                pl.BlockSpec((1, LANES), const),
                pl.BlockSpec((1, LANES), const)]
    args = [xbc, xbc, xbc, dt_raw, conv_w, conv_b.reshape(1, cd), pad(dt_bias), pad(a_log)]
    scratch = [pltpu.VMEM((SSM_GROUPS, gn // SSM_GROUPS, di // SSM_GROUPS), F32),
               pltpu.VMEM((CHUNK, di), F32)]
    if fin is not None:
        y_fwd, z, d_skip_row, norm_w = fin
        in_specs += [pl.BlockSpec((CHUNK, di), lambda bi, c: (local_chunk_of(bi, c), 0)),
                     pl.BlockSpec((CHUNK, di), lambda bi, c: (chunk_of(bi, c), 0)),
                     pl.BlockSpec((1, di), const),
                     pl.BlockSpec((1, di), const)]
        args += [y_fwd, z, d_skip_row, norm_w.reshape(1, di)]
        scratch.append(pltpu.VMEM((CHUNK, di), F32))
    return pl.pallas_call(
        functools.partial(_ssd_kernel, nc=nc, di=di, gn=gn, reverse=reverse, finalize=fin is not None),
        grid=(b, nc),
        in_specs=in_specs,
        out_specs=pl.BlockSpec((CHUNK, di), lambda bi, c: (local_chunk_of(bi, c), 0)),
        out_shape=jax.ShapeDtypeStruct((b * s, di), BF16 if fin is not None else F32),
        scratch_shapes=scratch,
        compiler_params=_params(("parallel", "arbitrary")),
        name=f"ssd_{'bwd' if reverse else 'fwd'}_s{s}",
    )(*args)


def _layer_weights(p):
    heads = p["dt_bias"].shape[-1]
    di = p["ssm_norm"].shape[-1]
    cd = p["conv_w"].shape[-1]
    d = p["w_in"].shape[0]
    w_in = p["w_in"]
    o_z, o_xbc, o_dt, o_gate = 3 * ATT_W, 3 * ATT_W + di, 3 * ATT_W + di + cd, 3 * ATT_W + di + cd + 2 * heads
    w_dt = jnp.zeros((d, 2 * LANES), F32)
    w_dt = w_dt.at[:, :heads].set(w_in[:, o_dt:o_dt + heads])
    w_dt = w_dt.at[:, LANES:LANES + heads].set(w_in[:, o_dt + heads:o_gate])
    bf = lambda a: a.astype(BF16)
    return dict(
        p,
        ffn1_w_gate=bf(p["ffn1_w_gate"]), ffn1_w_up=bf(p["ffn1_w_up"]), ffn1_w_down=bf(p["ffn1_w_down"]),
        ffn2_w_gate=bf(p["ffn2_w_gate"]), ffn2_w_up=bf(p["ffn2_w_up"]), ffn2_w_down=bf(p["ffn2_w_down"]),
        w_qkv=bf(w_in[:, :o_z]), w_z=bf(w_in[:, o_z:o_xbc]), w_xbc=bf(w_in[:, o_xbc:o_dt]),
        w_dt=bf(w_dt), w_gate=bf(w_in[:, o_gate:]),
        w_attn_out=bf(p["w_attn_out"]), w_ssm_out=bf(p["w_ssm_out"]), w_out=bf(p["w_out"]),
        d_skip_row=jnp.repeat(p["d_skip"], SSM_HEADDIM).reshape(1, di),
    )


def _ffn(x, norm_w, w_gate, w_up, w_down):
    h = rmsnorm_bf16(x, norm_w)
    a = ffn_up(h, w_gate, w_up)
    return matmul_residual(a, w_down, x, 0.5, name="ffn_down")


def _layer(x, w, sets, ropes):
    x = _ffn(x, w["ffn1_norm"], w["ffn1_w_gate"], w["ffn1_w_up"], w["ffn1_w_down"])
    h = rmsnorm_bf16(x, w["mix_norm"])
    qkv = matmul(h, w["w_qkv"], tn=768, name="proj_qkv")
    z = matmul(h, w["w_z"], name="proj_z")
    xbc = matmul(h, w["w_xbc"], name="proj_xbc")
    dt_raw = matmul(h, w["w_dt"], name="proj_dt")
    gate_raw = matmul(h, w["w_gate"], name="proj_gate")
    y_att, y_ssm = [], []
    for (row0, b, s), (cos, sin) in zip(sets, ropes):
        y_att.append(attention(qkv, cos, sin, w["q_norm"], w["k_norm"], row0, b, s))
        scan = functools.partial(ssd_scan, xbc, dt_raw, w["conv_w"], w["conv_b"], row0=row0, b=b, s=s)
        y_fwd = scan(dt_bias=w["dt_bias"][0], a_log=w["a_log"][0], reverse=False)
        y_ssm.append(scan(dt_bias=w["dt_bias"][1], a_log=w["a_log"][1], reverse=True,
                          fin=(y_fwd, z, w["d_skip_row"], w["ssm_norm"])))
    mix = gated_mix(jnp.concatenate(y_att), jnp.concatenate(y_ssm),
                    w["w_attn_out"], w["w_ssm_out"], gate_raw)
    x = matmul_residual(mix, w["w_out"], x, 1.0, name="mix_out")
    return _ffn(x, w["ffn2_norm"], w["ffn2_w_gate"], w["ffn2_w_up"], w["ffn2_w_down"])


def kernel(x_prompt, x_sample, ffn1_norm, ffn1_w_gate, ffn1_w_up, ffn1_w_down, mix_norm, w_in, q_norm, k_norm, conv_w, conv_b, dt_bias, a_log, d_skip, ssm_norm, w_attn_out, w_ssm_out, w_out, ffn2_norm, ffn2_w_gate, ffn2_w_up, ffn2_w_down):
    params = dict(ffn1_norm=ffn1_norm, ffn1_w_gate=ffn1_w_gate, ffn1_w_up=ffn1_w_up, ffn1_w_down=ffn1_w_down,
                  mix_norm=mix_norm, w_in=w_in, q_norm=q_norm, k_norm=k_norm, conv_w=conv_w, conv_b=conv_b,
                  dt_bias=dt_bias, a_log=a_log, d_skip=d_skip, ssm_norm=ssm_norm, w_attn_out=w_attn_out,
                  w_ssm_out=w_ssm_out, w_out=w_out, ffn2_norm=ffn2_norm, ffn2_w_gate=ffn2_w_gate,
                  ffn2_w_up=ffn2_w_up, ffn2_w_down=ffn2_w_down)
    d = x_prompt.shape[-1]
    bp, sp = x_prompt.shape[:2]
    bs, ss = x_sample.shape[:2]
    mp = bp * sp
    sets = ((0, bp, sp), (mp, bs, ss))
    ropes = (rope_tables(sp), rope_tables(ss))
    x = jnp.concatenate([x_prompt.reshape(mp, d), x_sample.reshape(bs * ss, d)])
    for layer in range(ffn1_norm.shape[0]):
        x = _layer(x, _layer_weights({k: v[layer] for k, v in params.items()}), sets, ropes)
    return x[:mp].reshape(x_prompt.shape), x[mp:].reshape(x_sample.shape)
```

```python
import functools

import jax
import jax.numpy as jnp
from jax import lax
from jax.experimental import pallas as pl
from jax.experimental.pallas import tpu as pltpu

HEAD_DIM = 128
ATTN_GROUPS = ((128, 1), (512, 4), (2048, 16))
HEADS_PER_GROUP = 4
ATT_HEADS = HEADS_PER_GROUP * len(ATTN_GROUPS)
ATT_W = ATT_HEADS * HEAD_DIM
ATT_OUT = HEADS_PER_GROUP * HEAD_DIM
ATT_RADIUS = 64
ROPE_THETA = 10000.0
SSM_GROUPS = 4
SSM_HEADDIM = 64
CHUNK = 128
RMS_EPS = 1e-6
NEG_INF = -1e30

LANES = 128
SUBLANES = 8
V7X_VMEM_LIMIT = 56 * 1024 * 1024

F32 = jnp.float32
BF16 = jnp.bfloat16

assert all(w // (2 * d) == ATT_RADIUS for w, d in ATTN_GROUPS)


def _params(semantics):
    return pltpu.CompilerParams(dimension_semantics=semantics, vmem_limit_bytes=V7X_VMEM_LIMIT)


def _tile(n, pref):
    t = min(pref, n)
    t -= t % LANES
    while n % t:
        t -= LANES
    return t


def _resident(shape):
    return pl.BlockSpec(shape, lambda *_: (0,) * len(shape), pipeline_mode=pl.Buffered(1))


def _row_rms(x):
    return lax.rsqrt(jnp.mean(x * x, axis=-1, keepdims=True) + RMS_EPS)


def _norm_prep_kernel(x_ref, w_ref, xw_ref, r_ref):
    x = x_ref[...]
    xw_ref[...] = (x * w_ref[...]).astype(BF16)
    r_ref[...] = jnp.broadcast_to(_row_rms(x), r_ref.shape)


def norm_prep(x, w, tm=256):
    m, d = x.shape
    return pl.pallas_call(
        _norm_prep_kernel,
        grid=(m // tm,),
        in_specs=[pl.BlockSpec((tm, d), lambda i: (i, 0)), _resident((1, d))],
        out_specs=[pl.BlockSpec((tm, d), lambda i: (i, 0)), pl.BlockSpec((tm, LANES), lambda i: (i, 0))],
        out_shape=[jax.ShapeDtypeStruct((m, d), BF16), jax.ShapeDtypeStruct((m, LANES), F32)],
        compiler_params=_params(("parallel",)),
        name="norm_prep",
    )(x, w.reshape(1, d))


def _mm_kernel(a_ref, r_ref, w_ref, o_ref):
    y = jnp.dot(a_ref[...], w_ref[...], preferred_element_type=F32)
    o_ref[...] = (y * r_ref[:, 0:1]).astype(o_ref.dtype)


def matmul_scaled(a, r, w, tm=1024, tn=1024, name="matmul"):
    m, k = a.shape
    n = w.shape[1]
    tm, tn = min(tm, m), _tile(n, tn)
    return pl.pallas_call(
        _mm_kernel,
        grid=(m // tm, n // tn),
        in_specs=[pl.BlockSpec((tm, k), lambda i, j: (i, 0)),
                  pl.BlockSpec((tm, LANES), lambda i, j: (i, 0)),
                  pl.BlockSpec((k, tn), lambda i, j: (0, j))],
        out_specs=pl.BlockSpec((tm, tn), lambda i, j: (i, j)),
        out_shape=jax.ShapeDtypeStruct((m, n), F32),
        compiler_params=_params(("parallel", "arbitrary")),
        name=name,
    )(a, r, w)


def _ffn_up_kernel(h_ref, r_ref, wg_ref, wu_ref, o_ref):
    h = h_ref[...]
    r = r_ref[:, 0:1]
    g = jnp.dot(h, wg_ref[...], preferred_element_type=F32) * r
    u = jnp.dot(h, wu_ref[...], preferred_element_type=F32) * r
    o_ref[...] = (g * jax.nn.sigmoid(g) * u).astype(o_ref.dtype)


def ffn_up(h, r, wg, wu, tm=1024, tn=512):
    m, k = h.shape
    n = wg.shape[1]
    tm, tn = min(tm, m), _tile(n, tn)
    return pl.pallas_call(
        _ffn_up_kernel,
        grid=(m // tm, n // tn),
        in_specs=[pl.BlockSpec((tm, k), lambda i, j: (i, 0)),
                  pl.BlockSpec((tm, LANES), lambda i, j: (i, 0)),
                  pl.BlockSpec((k, tn), lambda i, j: (0, j)),
                  pl.BlockSpec((k, tn), lambda i, j: (0, j))],
        out_specs=pl.BlockSpec((tm, tn), lambda i, j: (i, j)),
        out_shape=jax.ShapeDtypeStruct((m, n), BF16),
        compiler_params=_params(("parallel", "arbitrary")),
        name="ffn_up",
    )(h, r, wg, wu)


def _ffn_down_kernel(a_ref, w_ref, x_ref, *rest, scale, emit_norm):
    xn = x_ref[...] + scale * jnp.dot(a_ref[...], w_ref[...], preferred_element_type=F32)
    if emit_norm:
        wn_ref, o_ref, xw_ref, r_ref = rest
        xw_ref[...] = (xn * wn_ref[...]).astype(BF16)
        r_ref[...] = jnp.broadcast_to(_row_rms(xn), r_ref.shape)
    else:
        (o_ref,) = rest
    o_ref[...] = xn


def ffn_down(a, w, x, scale, next_norm_w=None, tm=256):
    m, k = a.shape
    n = w.shape[1]
    emit = next_norm_w is not None
    row = lambda i: (i, 0)
    in_specs = [pl.BlockSpec((tm, k), row), _resident((k, n)), pl.BlockSpec((tm, n), row)]
    out_specs = [pl.BlockSpec((tm, n), row)]
    out_shape = [jax.ShapeDtypeStruct((m, n), F32)]
    args = [a, w, x]
    if emit:
        in_specs.append(_resident((1, n)))
        args.append(next_norm_w.reshape(1, n))
        out_specs += [pl.BlockSpec((tm, n), row), pl.BlockSpec((tm, LANES), row)]
        out_shape += [jax.ShapeDtypeStruct((m, n), BF16), jax.ShapeDtypeStruct((m, LANES), F32)]
    out = pl.pallas_call(
        functools.partial(_ffn_down_kernel, scale=scale, emit_norm=emit),
        grid=(m // tm,),
        in_specs=in_specs,
        out_specs=out_specs,
        out_shape=out_shape,
        compiler_params=_params(("parallel",)),
        name="ffn_down",
    )(*args)
    return out if emit else out[0]


def _mix_out_kernel(a_ref, w_ref, x_ref, wn_ref, o_ref, xw_ref, r_ref, ssq_ref, *, d):
    j = pl.program_id(1)
    xn = x_ref[...] + jnp.dot(a_ref[...], w_ref[...], preferred_element_type=F32)
    o_ref[...] = xn
    xw_ref[...] = (xn * wn_ref[...]).astype(BF16)
    part = jnp.broadcast_to(jnp.sum(xn * xn, axis=-1, keepdims=True), ssq_ref.shape)

    @pl.when(j == 0)
    def _():
        ssq_ref[...] = part

    @pl.when(j > 0)
    def _():
        ssq_ref[...] += part

    @pl.when(j == pl.num_programs(1) - 1)
    def _():
        r_ref[...] = lax.rsqrt(ssq_ref[...] / d + RMS_EPS)


def mix_out(a, w, x, next_norm_w, tm=1024, tn=512):
    m, k = a.shape
    n = w.shape[1]
    tm, tn = min(tm, m), _tile(n, tn)
    return pl.pallas_call(
        functools.partial(_mix_out_kernel, d=n),
        grid=(m // tm, n // tn),
        in_specs=[pl.BlockSpec((tm, k), lambda i, j: (i, 0)),
                  pl.BlockSpec((k, tn), lambda i, j: (0, j)),
                  pl.BlockSpec((tm, tn), lambda i, j: (i, j)),
                  pl.BlockSpec((1, tn), lambda i, j: (0, j))],
        out_specs=[pl.BlockSpec((tm, tn), lambda i, j: (i, j)),
                   pl.BlockSpec((tm, tn), lambda i, j: (i, j)),
                   pl.BlockSpec((tm, LANES), lambda i, j: (i, 0))],
        out_shape=[jax.ShapeDtypeStruct((m, n), F32), jax.ShapeDtypeStruct((m, n), BF16),
                   jax.ShapeDtypeStruct((m, LANES), F32)],
        scratch_shapes=[pltpu.VMEM((tm, LANES), F32)],
        compiler_params=_params(("parallel", "arbitrary")),
        name="mix_out",
    )(a, w, x, next_norm_w.reshape(1, n))


def _gated_mix_kernel(ya_ref, ys_ref, wa_ref, ws_ref, ga_ref, gs_ref, o_ref):
    pa = jnp.dot(ya_ref[...], wa_ref[...], preferred_element_type=F32)
    ps = jnp.dot(ys_ref[...], ws_ref[...], preferred_element_type=F32)
    o_ref[...] = (jax.nn.sigmoid(ga_ref[...]) * pa + jax.nn.sigmoid(gs_ref[...]) * ps).astype(o_ref.dtype)


def gated_mix(ya, ys, wa, ws, gate_raw, tm=256):
    m, ka = ya.shape
    ks = ys.shape[1]
    n = wa.shape[1]
    row = lambda i: (i, 0)
    return pl.pallas_call(
        _gated_mix_kernel,
        grid=(m // tm,),
        in_specs=[pl.BlockSpec((tm, ka), row), pl.BlockSpec((tm, ks), row),
                  _resident((ka, n)), _resident((ks, n)),
                  pl.BlockSpec((tm, n), row), pl.BlockSpec((tm, n), lambda i: (i, 1))],
        out_specs=pl.BlockSpec((tm, n), row),
        out_shape=jax.ShapeDtypeStruct((m, n), BF16),
        compiler_params=_params(("parallel",)),
        name="gated_mix",
    )(ya, ys, wa, ws, gate_raw, gate_raw)


ATT_TQ = 128
ATT_TK = ATT_TQ + 2 * ATT_RADIUS
ATT_UNROLL = 4


def _attn_group(q_ref, k_ref, v_ref, cos_ref, sin_ref, qw_ref, kw_ref,
                qd_ref, kd_ref, vd_ref, bias_ref, acc_ref, m_ref, l_ref, *, s, dil, first):
    n = s // dil
    nblk = n // ATT_TQ
    seg = n + 2 * ATT_RADIUS
    scale = HEAD_DIM ** -0.5

    def rows_of(c):
        r, mb = c // nblk, c % nblk
        if dil == 1:
            return r, mb, pl.ds(pl.multiple_of(c * ATT_TQ, ATT_TQ), ATT_TQ)
        return r, mb, pl.ds(r + mb * (ATT_TQ * dil), ATT_TQ, stride=dil)

    def norm_rope(x, w_ref, rows, out_scale):
        w, w_rolled = w_ref[0:1, :], w_ref[1:2, :]
        xr = pltpu.roll(x, HEAD_DIM // 2, 1)
        y = x * (cos_ref[rows, :] * w) + xr * (sin_ref[rows, :] * w_rolled)
        return y * (_row_rms(x) * out_scale)

    zeros = jnp.zeros((ATT_RADIUS, 2 * HEAD_DIM), BF16)
    ones = jnp.ones((ATT_TQ, HEAD_DIM), BF16)

    def zero_pads(r, carry):
        lo = pl.multiple_of(r * seg, ATT_RADIUS)
        hi = pl.multiple_of(r * seg + ATT_RADIUS + n, ATT_RADIUS)
        kd_ref[pl.ds(lo, ATT_RADIUS), :] = zeros[:, :HEAD_DIM]
        kd_ref[pl.ds(hi, ATT_RADIUS), :] = zeros[:, :HEAD_DIM]
        vd_ref[pl.ds(lo, ATT_RADIUS), :] = zeros
        vd_ref[pl.ds(hi, ATT_RADIUS), :] = zeros
        return carry

    lax.fori_loop(0, dil, zero_pads, 0)

    def prep(c2, carry):
        for u in range(ATT_UNROLL):
            c = c2 * ATT_UNROLL + u
            r, mb, rows = rows_of(c)
            qn = norm_rope(q_ref[rows, :], qw_ref, rows, scale)
            kn = norm_rope(k_ref[rows, :], kw_ref, rows, 1.0)
            qd_ref[pl.ds(pl.multiple_of(c * ATT_TQ, ATT_TQ), ATT_TQ), :] = qn.astype(BF16)
            dst = pl.ds(pl.multiple_of(r * seg + ATT_RADIUS + mb * ATT_TQ, ATT_RADIUS), ATT_TQ)
            kd_ref[dst, :] = kn.astype(BF16)
            vd_ref[dst, :HEAD_DIM] = v_ref[rows, :].astype(BF16)
            vd_ref[dst, HEAD_DIM:] = ones
        return carry

    lax.fori_loop(0, s // (ATT_TQ * ATT_UNROLL), prep, 0)

    def block(c2, carry):
        parts = []
        for u in range(ATT_UNROLL):
            c = c2 * ATT_UNROLL + u
            r, mb, rows = rows_of(c)
            qn = qd_ref[pl.ds(pl.multiple_of(c * ATT_TQ, ATT_TQ), ATT_TQ), :]
            win = pl.ds(pl.multiple_of(r * seg + mb * ATT_TQ, ATT_RADIUS), ATT_TK)
            edge = (mb == 0).astype(jnp.int32) + 2 * (mb == nblk - 1).astype(jnp.int32)
            sc = lax.dot_general(qn, kd_ref[win, :], (((1,), (1,)), ((), ())), preferred_element_type=F32)
            sc = sc + bias_ref[edge]
            mx = jnp.max(sc, axis=-1, keepdims=True)
            p = jnp.exp(sc - mx).astype(BF16)
            pv = jnp.dot(p, vd_ref[win, :], preferred_element_type=F32)
            parts.append((rows, mx, pv))
        for rows, mx, pv in parts:
            m_b = jnp.broadcast_to(mx, (ATT_TQ, HEAD_DIM))
            pv, l_b = pv[:, :HEAD_DIM], pv[:, HEAD_DIM:]
            if first:
                acc_n, m_n, l_n = pv, m_b, l_b
            else:
                m_o = m_ref[rows, :]
                m_n = jnp.maximum(m_o, m_b)
                a, b = jnp.exp(m_o - m_n), jnp.exp(m_b - m_n)
                acc_n = a * acc_ref[rows, :] + b * pv
                l_n = a * l_ref[rows, :] + b * l_b
            acc_ref[rows, :] = acc_n
            m_ref[rows, :] = m_n
            l_ref[rows, :] = l_n
        return carry

    lax.fori_loop(0, s // (ATT_TQ * ATT_UNROLL), block, 0)


def _attn_kernel(q_ref, k_ref, v_ref, cos_ref, sin_ref, qw_ref, kw_ref, o_ref,
                 qd_ref, kd_ref, vd_ref, bias_ref, acc_ref, m_ref, l_ref, *, s):
    g = pl.program_id(2)
    ng = len(ATTN_GROUPS)

    qi = lax.broadcasted_iota(jnp.int32, (ATT_TQ, ATT_TK), 0)
    kj = lax.broadcasted_iota(jnp.int32, (ATT_TQ, ATT_TK), 1)
    band = jnp.abs(kj - ATT_RADIUS - qi) <= ATT_RADIUS
    for e in range(4):
        ok = band
        if e & 1:
            ok = ok & (kj >= ATT_RADIUS)
        if e & 2:
            ok = ok & (kj < ATT_TQ + ATT_RADIUS)
        bias_ref[e] = jnp.where(ok, 0.0, NEG_INF)

    for step, (_, dil) in enumerate(reversed(ATTN_GROUPS)):
        @pl.when(g == step)
        def _(dil=dil, step=step):
            _attn_group(q_ref, k_ref, v_ref, cos_ref, sin_ref, qw_ref, kw_ref,
                        qd_ref, kd_ref, vd_ref, bias_ref, acc_ref, m_ref, l_ref,
                        s=s, dil=dil, first=step == 0)

    @pl.when(g == ng - 1)
    def _():
        rows_per = 2 * ATT_TQ

        def finish(c, carry):
            rows = pl.ds(pl.multiple_of(c * rows_per, rows_per), rows_per)
            o_ref[rows, :] = (acc_ref[rows, :] / l_ref[rows, :]).astype(o_ref.dtype)
            return carry

        lax.fori_loop(0, s // rows_per, finish, 0)


def attention(qkv, cos, sin, q_norm, k_norm, row0, b, s):
    blk0 = row0 // s
    hpg = HEADS_PER_GROUP
    max_dil = max(d for _, d in ATTN_GROUPS)
    kv_rows = s + 2 * ATT_RADIUS * max_dil

    def col(base):
        return lambda bi, j, g: (blk0 + bi, base + (len(ATTN_GROUPS) - 1 - g) * hpg + j)

    def with_rolled(w):
        return jnp.stack([w, jnp.roll(w, HEAD_DIM // 2)])

    return pl.pallas_call(
        functools.partial(_attn_kernel, s=s),
        grid=(b, hpg, len(ATTN_GROUPS)),
        in_specs=[pl.BlockSpec((s, HEAD_DIM), col(0)),
                  pl.BlockSpec((s, HEAD_DIM), col(ATT_HEADS)),
                  pl.BlockSpec((s, HEAD_DIM), col(2 * ATT_HEADS)),
                  _resident((s, HEAD_DIM)),
                  _resident((s, HEAD_DIM)),
                  _resident((2, HEAD_DIM)),
                  _resident((2, HEAD_DIM))],
        out_specs=pl.BlockSpec((s, HEAD_DIM), lambda bi, j, g: (bi, j)),
        out_shape=jax.ShapeDtypeStruct((b * s, ATT_OUT), BF16),
        scratch_shapes=[pltpu.VMEM((s, HEAD_DIM), BF16),
                        pltpu.VMEM((kv_rows, HEAD_DIM), BF16),
                        pltpu.VMEM((kv_rows, 2 * HEAD_DIM), BF16),
                        pltpu.VMEM((4, ATT_TQ, ATT_TK), F32),
                        pltpu.VMEM((s, HEAD_DIM), F32),
                        pltpu.VMEM((s, HEAD_DIM), F32),
                        pltpu.VMEM((s, HEAD_DIM), F32)],
        compiler_params=_params(("parallel", "parallel", "arbitrary")),
        name=f"dilated_attention_s{s}",
    )(qkv, qkv, qkv, cos, sin, with_rolled(q_norm), with_rolled(k_norm))


def rope_tables(s):
    inv_freq = ROPE_THETA ** (-jnp.arange(0, HEAD_DIM, 2, dtype=F32) / HEAD_DIM)
    ang = jnp.arange(s, dtype=F32)[:, None] * inv_freq[None, :]
    cos, sin = jnp.cos(ang), jnp.sin(ang)
    return jnp.concatenate([cos, cos], axis=-1), jnp.concatenate([-sin, sin], axis=-1)


def _scan_rows(x, reverse):
    row = lax.broadcasted_iota(jnp.int32, x.shape, 0)
    sh = 1
    while sh < CHUNK:
        if reverse:
            x = x + jnp.where(row < CHUNK - sh, pltpu.roll(x, CHUNK - sh, 0), 0.0)
        else:
            x = x + jnp.where(row >= sh, pltpu.roll(x, sh, 0), 0.0)
        sh *= 2
    return x


def _ssd_fwd_kernel(xbc_ref, xprev_ref, xnext_ref, dt_ref, cw_ref, cb_ref, dtb_ref, alog_ref,
                    y_ref, xs_ref, bc_ref, state_ref, *, nc, di, gn):
    chunk = pl.program_id(1)
    x = xbc_ref[...]
    row = lax.broadcasted_iota(jnp.int32, (SUBLANES, 1), 0)
    prev_row = jnp.where(chunk > 0, xprev_ref[SUBLANES - 1:SUBLANES, :], 0.0)
    next_row = jnp.where(chunk < nc - 1, xnext_ref[0:1, :], 0.0)
    xm1 = pltpu.roll(x, 1, 0)
    xm1 = jnp.concatenate([jnp.where(row == 0, prev_row, xm1[:SUBLANES]), xm1[SUBLANES:]], axis=0)
    xp1 = pltpu.roll(x, CHUNK - 1, 0)
    xp1 = jnp.concatenate([xp1[:-SUBLANES], jnp.where(row == SUBLANES - 1, next_row, xp1[-SUBLANES:])], axis=0)
    xc = xm1 * cw_ref[0:1, :] + x * cw_ref[1:2, :] + xp1 * cw_ref[2:3, :] + cb_ref[...]
    xc = xc * jax.nn.sigmoid(xc)
    xs_ref[...] = xc[:, :di]
    bc_ref[...] = xc[:, di:].astype(BF16)
    _ssd_chunk(xs_ref, bc_ref, dt_ref, dtb_ref, alog_ref, y_ref, state_ref, di=di, gn=gn, reverse=False)


def _ssd_bwd_kernel(xs_ref, bc_ref, dt_ref, dtb_ref, alog_ref, yf_ref, z_ref, dskip_ref, nw_ref,
                    o_ref, state_ref, y_ref, *, di, gn):
    _ssd_chunk(xs_ref, bc_ref, dt_ref, dtb_ref, alog_ref, y_ref, state_ref, di=di, gn=gn, reverse=True)
    y = yf_ref[...] + y_ref[...] + dskip_ref[...] * xs_ref[...]
    z = z_ref[...]
    y = y * (z * jax.nn.sigmoid(z))
    o_ref[...] = ((y * _row_rms(y)) * nw_ref[...]).astype(o_ref.dtype)


def _ssd_chunk(xs_ref, bc_ref, dt_ref, dtb_ref, alog_ref, y_ref, state_ref, *, di, gn, reverse):
    gw = di // SSM_GROUPS
    n_state = gn // SSM_GROUPS
    half = SSM_HEADDIM

    @pl.when(pl.program_id(1) == 0)
    def _():
        state_ref[...] = jnp.zeros_like(state_ref)

    dtr = dt_ref[...] + dtb_ref[...]
    dt = jnp.maximum(dtr, 0.0) + jnp.log1p(jnp.exp(-jnp.abs(dtr)))
    acum = _scan_rows(dt * (-jnp.exp(alog_ref[...])), reverse)
    total = acum[0:1, :] if reverse else acum[CHUNK - 1:CHUNK, :]
    wend = jnp.exp(total - acum) * dt
    acum_t = acum.T
    dt_t = dt.T

    li = lax.broadcasted_iota(jnp.int32, (CHUNK, CHUNK), 0)
    si = lax.broadcasted_iota(jnp.int32, (CHUNK, CHUNK), 1)
    causal = (si >= li) if reverse else (li >= si)
    lane = lax.broadcasted_iota(jnp.int32, (CHUNK, LANES), 1)
    lo_half = lane < half

    def lanes_of(col_vals, h):
        return jnp.broadcast_to(col_vals[:, h:h + 1], (CHUNK, LANES))

    for g in range(SSM_GROUPS):
        b_g = bc_ref[:, g * n_state:(g + 1) * n_state]
        c_g = bc_ref[:, gn + g * n_state:gn + (g + 1) * n_state]
        cb = lax.dot_general(c_g, b_g, (((1,), (1,)), ((), ())), preferred_element_type=F32)
        st = state_ref[g]
        y_off = jnp.dot(c_g, st.astype(BF16), preferred_element_type=F32)
        wx_parts, dec_parts = [], []
        for pr in range(gw // LANES):
            col0 = g * gw + pr * LANES
            h0 = col0 // half
            xp = xs_ref[:, col0:col0 + LANES]
            sc, ecol, wcol = [], [], []
            for h in (h0, h0 + 1):
                a_col = lanes_of(acum, h)
                dec = jnp.exp(jnp.where(causal, a_col - acum_t[h:h + 1, :], -jnp.inf))
                sc.append((cb * dec * dt_t[h:h + 1, :]).astype(BF16))
                ecol.append(jnp.exp(a_col))
                wcol.append(lanes_of(wend, h))
            s2 = jnp.concatenate(sc, axis=1)
            x2 = jnp.concatenate([jnp.where(lo_half, xp, 0.0), jnp.where(lo_half, 0.0, xp)], axis=0).astype(BF16)
            y = jnp.dot(s2, x2, preferred_element_type=F32)
            y = y + y_off[:, pr * LANES:(pr + 1) * LANES] * jnp.where(lo_half, ecol[0], ecol[1])
            y_ref[:, col0:col0 + LANES] = y
            wx_parts.append((jnp.where(lo_half, wcol[0], wcol[1]) * xp).astype(BF16))
            tot0 = jnp.broadcast_to(total[:, h0:h0 + 1], (1, LANES))
            tot1 = jnp.broadcast_to(total[:, h0 + 1:h0 + 2], (1, LANES))
            dec_parts.append(jnp.exp(jnp.where(lo_half[0:1, :], tot0, tot1)))
        wx = jnp.concatenate(wx_parts, axis=1) if len(wx_parts) > 1 else wx_parts[0]
        sdec = jnp.concatenate(dec_parts, axis=1) if len(dec_parts) > 1 else dec_parts[0]
        upd = lax.dot_general(b_g, wx, (((0,), (0,)), ((), ())), preferred_element_type=F32)
        state_ref[g] = st * sdec + upd


def ssd_mixer(xbc, dt_raw, z, conv_w, conv_b, dt_bias, a_log, d_skip_row, norm_w, row0, b, s):
    cd = conv_w.shape[1]
    heads = dt_bias.shape[-1]
    di = heads * SSM_HEADDIM
    gn = (cd - di) // 2
    nc = s // CHUNK
    cblk0 = row0 // CHUNK
    per8 = CHUNK // SUBLANES
    last8 = xbc.shape[0] // SUBLANES - 1
    pad = lambda v: jnp.zeros((1, LANES), F32).at[0, :heads].set(v)
    state = pltpu.VMEM((SSM_GROUPS, gn // SSM_GROUPS, di // SSM_GROUPS), F32)

    fwd_g = lambda bi, c: (cblk0 + bi * nc + c, 0)
    fwd_l = lambda bi, c: (bi * nc + c, 0)
    y_fwd, xs, bc = pl.pallas_call(
        functools.partial(_ssd_fwd_kernel, nc=nc, di=di, gn=gn),
        grid=(b, nc),
        in_specs=[pl.BlockSpec((CHUNK, cd), fwd_g),
                  pl.BlockSpec((SUBLANES, cd), lambda bi, c: (jnp.maximum(fwd_g(bi, c)[0] * per8 - 1, 0), 0)),
                  pl.BlockSpec((SUBLANES, cd), lambda bi, c: (jnp.minimum((fwd_g(bi, c)[0] + 1) * per8, last8), 0)),
                  pl.BlockSpec((CHUNK, LANES), fwd_g),
                  _resident((3, cd)), _resident((1, cd)), _resident((1, LANES)), _resident((1, LANES))],
        out_specs=[pl.BlockSpec((CHUNK, di), fwd_l), pl.BlockSpec((CHUNK, di), fwd_l),
                   pl.BlockSpec((CHUNK, 2 * gn), fwd_l)],
        out_shape=[jax.ShapeDtypeStruct((b * s, di), F32), jax.ShapeDtypeStruct((b * s, di), F32),
                   jax.ShapeDtypeStruct((b * s, 2 * gn), BF16)],
        scratch_shapes=[state],
        compiler_params=_params(("parallel", "arbitrary")),
        name=f"ssd_fwd_s{s}",
    )(xbc, xbc, xbc, dt_raw, conv_w, conv_b.reshape(1, cd), pad(dt_bias[0]), pad(a_log[0]))

    bwd_g = lambda bi, c: (cblk0 + bi * nc + nc - 1 - c, 0)
    bwd_l = lambda bi, c: (bi * nc + nc - 1 - c, 0)
    return pl.pallas_call(
        functools.partial(_ssd_bwd_kernel, di=di, gn=gn),
        grid=(b, nc),
        in_specs=[pl.BlockSpec((CHUNK, di), bwd_l), pl.BlockSpec((CHUNK, 2 * gn), bwd_l),
                  pl.BlockSpec((CHUNK, LANES), lambda bi, c: (bwd_g(bi, c)[0], 1)),
                  _resident((1, LANES)), _resident((1, LANES)),
                  pl.BlockSpec((CHUNK, di), bwd_l), pl.BlockSpec((CHUNK, di), bwd_g),
                  _resident((1, di)), _resident((1, di))],
        out_specs=pl.BlockSpec((CHUNK, di), bwd_l),
        out_shape=jax.ShapeDtypeStruct((b * s, di), BF16),
        scratch_shapes=[state, pltpu.VMEM((CHUNK, di), F32)],
        compiler_params=_params(("parallel", "arbitrary")),
        name=f"ssd_bwd_s{s}",
    )(xs, bc, dt_raw, pad(dt_bias[1]), pad(a_log[1]), y_fwd, z, d_skip_row, norm_w.reshape(1, di))


def _layer_weights(p):
    heads = p["dt_bias"].shape[-1]
    di = p["ssm_norm"].shape[-1]
    cd = p["conv_w"].shape[-1]
    d = p["w_in"].shape[0]
    w_in = p["w_in"]
    o_z, o_xbc, o_dt, o_gate = 3 * ATT_W, 3 * ATT_W + di, 3 * ATT_W + di + cd, 3 * ATT_W + di + cd + 2 * heads
    w_dt = jnp.zeros((d, 2 * LANES), F32)
    w_dt = w_dt.at[:, :heads].set(w_in[:, o_dt:o_dt + heads])
    w_dt = w_dt.at[:, LANES:LANES + heads].set(w_in[:, o_dt + heads:o_gate])
    bf = lambda a: a.astype(BF16)
    return dict(
        p,
        ffn1_w_gate=bf(p["ffn1_w_gate"]), ffn1_w_up=bf(p["ffn1_w_up"]), ffn1_w_down=bf(p["ffn1_w_down"]),
        ffn2_w_gate=bf(p["ffn2_w_gate"]), ffn2_w_up=bf(p["ffn2_w_up"]), ffn2_w_down=bf(p["ffn2_w_down"]),
        w_qkv=bf(w_in[:, :o_z]), w_z=bf(w_in[:, o_z:o_xbc]), w_xbc=bf(w_in[:, o_xbc:o_dt]),
        w_dt=bf(w_dt), w_gate=bf(w_in[:, o_gate:]),
        w_attn_out=bf(p["w_attn_out"]), w_ssm_out=bf(p["w_ssm_out"]), w_out=bf(p["w_out"]),
        d_skip_row=jnp.repeat(p["d_skip"], SSM_HEADDIM).reshape(1, di),
    )


def _layer(x, xw, r, w, next_norm_w, sets, ropes):
    a = ffn_up(xw, r, w["ffn1_w_gate"], w["ffn1_w_up"])
    x, xw, r = ffn_down(a, w["ffn1_w_down"], x, 0.5, w["mix_norm"])
    qkv = matmul_scaled(xw, r, w["w_qkv"], tn=768, name="proj_qkv")
    z = matmul_scaled(xw, r, w["w_z"], name="proj_z")
    xbc = matmul_scaled(xw, r, w["w_xbc"], name="proj_xbc")
    dt_raw = matmul_scaled(xw, r, w["w_dt"], name="proj_dt")
    gate_raw = matmul_scaled(xw, r, w["w_gate"], name="proj_gate")
    y_att, y_ssm = [], []
    for (row0, b, s), (cos, sin) in zip(sets, ropes):
        y_att.append(attention(qkv, cos, sin, w["q_norm"], w["k_norm"], row0, b, s))
        y_ssm.append(ssd_mixer(xbc, dt_raw, z, w["conv_w"], w["conv_b"], w["dt_bias"], w["a_log"],
                               w["d_skip_row"], w["ssm_norm"], row0, b, s))
    mix = gated_mix(jnp.concatenate(y_att), jnp.concatenate(y_ssm),
                    w["w_attn_out"], w["w_ssm_out"], gate_raw)
    x, xw, r = mix_out(mix, w["w_out"], x, w["ffn2_norm"])
    a = ffn_up(xw, r, w["ffn2_w_gate"], w["ffn2_w_up"])
    if next_norm_w is None:
        return ffn_down(a, w["ffn2_w_down"], x, 0.5), None, None
    return ffn_down(a, w["ffn2_w_down"], x, 0.5, next_norm_w)


def kernel(x_prompt, x_sample, ffn1_norm, ffn1_w_gate, ffn1_w_up, ffn1_w_down, mix_norm, w_in, q_norm, k_norm, conv_w, conv_b, dt_bias, a_log, d_skip, ssm_norm, w_attn_out, w_ssm_out, w_out, ffn2_norm, ffn2_w_gate, ffn2_w_up, ffn2_w_down):
    params = dict(ffn1_norm=ffn1_norm, ffn1_w_gate=ffn1_w_gate, ffn1_w_up=ffn1_w_up, ffn1_w_down=ffn1_w_down,
                  mix_norm=mix_norm, w_in=w_in, q_norm=q_norm, k_norm=k_norm, conv_w=conv_w, conv_b=conv_b,
                  dt_bias=dt_bias, a_log=a_log, d_skip=d_skip, ssm_norm=ssm_norm, w_attn_out=w_attn_out,
                  w_ssm_out=w_ssm_out, w_out=w_out, ffn2_norm=ffn2_norm, ffn2_w_gate=ffn2_w_gate,
                  ffn2_w_up=ffn2_w_up, ffn2_w_down=ffn2_w_down)
    d = x_prompt.shape[-1]
    bp, sp = x_prompt.shape[:2]
    bs, ss = x_sample.shape[:2]
    mp = bp * sp
    depth = ffn1_norm.shape[0]
    sets = ((0, bp, sp), (mp, bs, ss))
    ropes = (rope_tables(sp), rope_tables(ss))
    x = jnp.concatenate([x_prompt.reshape(mp, d), x_sample.reshape(bs * ss, d)])
    xw, r = norm_prep(x, ffn1_norm[0])
    for layer in range(depth):
        w = _layer_weights({k: v[layer] for k, v in params.items()})
        next_norm_w = ffn1_norm[layer + 1] if layer + 1 < depth else None
        x, xw, r = _layer(x, xw, r, w, next_norm_w, sets, ropes)
    return x[:mp].reshape(x_prompt.shape), x[mp:].reshape(x_sample.shape)
```

```python
import functools

import jax
import jax.numpy as jnp
from jax import lax
from jax.experimental import pallas as pl
from jax.experimental.pallas import tpu as pltpu

HEAD_DIM = 128
ATTN_GROUPS = ((128, 1), (512, 4), (2048, 16))
HEADS_PER_GROUP = 4
ATT_HEADS = HEADS_PER_GROUP * len(ATTN_GROUPS)
ATT_W = ATT_HEADS * HEAD_DIM
ATT_OUT = HEADS_PER_GROUP * HEAD_DIM
ATT_RADIUS = 64
ROPE_THETA = 10000.0
SSM_GROUPS = 4
SSM_HEADDIM = 64
CHUNK = 128
RMS_EPS = 1e-6
NEG_INF = -1e30

LANES = 128
SUBLANES = 8
V7X_VMEM_LIMIT = 56 * 1024 * 1024

F32 = jnp.float32
BF16 = jnp.bfloat16

assert all(w // (2 * d) == ATT_RADIUS for w, d in ATTN_GROUPS)


def _params(semantics):
    return pltpu.CompilerParams(dimension_semantics=semantics, vmem_limit_bytes=V7X_VMEM_LIMIT)


def _tile(n, pref):
    t = min(pref, n)
    t -= t % LANES
    while n % t:
        t -= LANES
    return t


def _resident(shape):
    return pl.BlockSpec(shape, lambda *_: (0,) * len(shape), pipeline_mode=pl.Buffered(1))


class _SharedOut:
    def __init__(self, prev):
        self.args = [] if prev is None else [prev]
        self.in_specs = [pl.BlockSpec(memory_space=pl.ANY)] * len(self.args)

    def aliases(self, operand_index):
        return {operand_index: 0} if self.args else {}


def _row_rms(x):
    return lax.rsqrt(jnp.mean(x * x, axis=-1, keepdims=True) + RMS_EPS)


def _norm_prep_kernel(x_ref, w_ref, xw_ref, r_ref):
    x = x_ref[...]
    xw_ref[...] = (x * w_ref[...]).astype(BF16)
    r_ref[...] = jnp.broadcast_to(_row_rms(x), r_ref.shape)


def norm_prep(x, w, tm=256):
    m, d = x.shape
    return pl.pallas_call(
        _norm_prep_kernel,
        grid=(m // tm,),
        in_specs=[pl.BlockSpec((tm, d), lambda i: (i, 0)), _resident((1, d))],
        out_specs=[pl.BlockSpec((tm, d), lambda i: (i, 0)), pl.BlockSpec((tm, LANES), lambda i: (i, 0))],
        out_shape=[jax.ShapeDtypeStruct((m, d), BF16), jax.ShapeDtypeStruct((m, LANES), F32)],
        compiler_params=_params(("parallel",)),
        name="norm_prep",
    )(x, w.reshape(1, d))


def _layer_cols(w, layer, tn, col0=0):
    assert col0 % tn == 0
    return pl.BlockSpec((None, w.shape[1], tn), lambda i, j: (layer, 0, col0 // tn + j))


def _mm_kernel(a_ref, r_ref, w_ref, o_ref):
    y = jnp.dot(a_ref[...], w_ref[...].astype(BF16), preferred_element_type=F32)
    o_ref[...] = (y * r_ref[:, 0:1]).astype(o_ref.dtype)


def matmul_scaled(a, r, w, layer=0, col0=0, n=None, tm=1024, tn=512, name="matmul"):
    m, k = a.shape
    n = w.shape[2] if n is None else n
    tm, tn = min(tm, m), _tile(n, tn)
    return pl.pallas_call(
        _mm_kernel,
        grid=(m // tm, n // tn),
        in_specs=[pl.BlockSpec((tm, k), lambda i, j: (i, 0)),
                  pl.BlockSpec((tm, LANES), lambda i, j: (i, 0)),
                  _layer_cols(w, layer, tn, col0)],
        out_specs=pl.BlockSpec((tm, tn), lambda i, j: (i, j)),
        out_shape=jax.ShapeDtypeStruct((m, n), F32),
        compiler_params=_params(("parallel", "arbitrary")),
        name=name,
    )(a, r, w)


def _ffn_up_kernel(h_ref, r_ref, wg_ref, wu_ref, o_ref):
    h = h_ref[...]
    r = r_ref[:, 0:1]
    g = jnp.dot(h, wg_ref[...].astype(BF16), preferred_element_type=F32) * r
    u = jnp.dot(h, wu_ref[...].astype(BF16), preferred_element_type=F32) * r
    o_ref[...] = (g * jax.nn.sigmoid(g) * u).astype(o_ref.dtype)


def ffn_up(h, r, wg, wu, layer, tm=1024, tn=256):
    m, k = h.shape
    n = wg.shape[2]
    tm, tn = min(tm, m), _tile(n, tn)
    return pl.pallas_call(
        _ffn_up_kernel,
        grid=(m // tm, n // tn),
        in_specs=[pl.BlockSpec((tm, k), lambda i, j: (i, 0)),
                  pl.BlockSpec((tm, LANES), lambda i, j: (i, 0)),
                  _layer_cols(wg, layer, tn),
                  _layer_cols(wu, layer, tn)],
        out_specs=pl.BlockSpec((tm, tn), lambda i, j: (i, j)),
        out_shape=jax.ShapeDtypeStruct((m, n), BF16),
        compiler_params=_params(("parallel", "arbitrary")),
        name="ffn_up",
    )(h, r, wg, wu)


def _ffn_down_kernel(a_ref, w_ref, x_ref, *rest, scale, emit_norm):
    xn = x_ref[...] + scale * jnp.dot(a_ref[...], w_ref[...], preferred_element_type=F32)
    if emit_norm:
        wn_ref, o_ref, xw_ref, r_ref = rest
        xw_ref[...] = (xn * wn_ref[...]).astype(BF16)
        r_ref[...] = jnp.broadcast_to(_row_rms(xn), r_ref.shape)
    else:
        (o_ref,) = rest
    o_ref[...] = xn


def ffn_down(a, w, x, scale, next_norm_w=None, row0=0, rows=None, tm=256):
    k = a.shape[1]
    m = a.shape[0] if rows is None else rows
    n = w.shape[1]
    emit = next_norm_w is not None
    row = lambda i: (i, 0)
    src_row = lambda i: (i + row0 // tm, 0)
    in_specs = [pl.BlockSpec((tm, k), src_row), _resident((k, n)), pl.BlockSpec((tm, n), src_row)]
    out_specs = [pl.BlockSpec((tm, n), row)]
    out_shape = [jax.ShapeDtypeStruct((m, n), F32)]
    args = [a, w, x]
    if emit:
        in_specs.append(_resident((1, n)))
        args.append(next_norm_w.reshape(1, n))
        out_specs += [pl.BlockSpec((tm, n), row), pl.BlockSpec((tm, LANES), row)]
        out_shape += [jax.ShapeDtypeStruct((m, n), BF16), jax.ShapeDtypeStruct((m, LANES), F32)]
    out = pl.pallas_call(
        functools.partial(_ffn_down_kernel, scale=scale, emit_norm=emit),
        grid=(m // tm,),
        in_specs=in_specs,
        out_specs=out_specs,
        out_shape=out_shape,
        compiler_params=_params(("parallel",)),
        name="ffn_down",
    )(*args)
    return out if emit else out[0]


def _mix_out_kernel(a_ref, w_ref, x_ref, wn_ref, o_ref, xw_ref, r_ref, ssq_ref, *, d):
    j = pl.program_id(1)
    xn = x_ref[...] + jnp.dot(a_ref[...], w_ref[...].astype(BF16), preferred_element_type=F32)
    o_ref[...] = xn
    xw_ref[...] = (xn * wn_ref[...]).astype(BF16)
    part = jnp.broadcast_to(jnp.sum(xn * xn, axis=-1, keepdims=True), ssq_ref.shape)

    @pl.when(j == 0)
    def _():
        ssq_ref[...] = part

    @pl.when(j > 0)
    def _():
        ssq_ref[...] += part

    @pl.when(j == pl.num_programs(1) - 1)
    def _():
        r_ref[...] = lax.rsqrt(ssq_ref[...] / d + RMS_EPS)


def mix_out(a, w, layer, x, next_norm_w, tm=1024, tn=512):
    m, k = a.shape
    n = w.shape[2]
    tm, tn = min(tm, m), _tile(n, tn)
    return pl.pallas_call(
        functools.partial(_mix_out_kernel, d=n),
        grid=(m // tm, n // tn),
        in_specs=[pl.BlockSpec((tm, k), lambda i, j: (i, 0)),
                  _layer_cols(w, layer, tn),
                  pl.BlockSpec((tm, tn), lambda i, j: (i, j)),
                  pl.BlockSpec((1, tn), lambda i, j: (0, j))],
        out_specs=[pl.BlockSpec((tm, tn), lambda i, j: (i, j)),
                   pl.BlockSpec((tm, tn), lambda i, j: (i, j)),
                   pl.BlockSpec((tm, LANES), lambda i, j: (i, 0))],
        out_shape=[jax.ShapeDtypeStruct((m, n), F32), jax.ShapeDtypeStruct((m, n), BF16),
                   jax.ShapeDtypeStruct((m, LANES), F32)],
        scratch_shapes=[pltpu.VMEM((tm, LANES), F32)],
        compiler_params=_params(("parallel", "arbitrary")),
        name="mix_out",
    )(a, w, x, next_norm_w.reshape(1, n))


def _gated_mix_kernel(ya_ref, ys_ref, wa_ref, ws_ref, ga_ref, gs_ref, o_ref):
    pa = jnp.dot(ya_ref[...], wa_ref[...], preferred_element_type=F32)
    ps = jnp.dot(ys_ref[...], ws_ref[...], preferred_element_type=F32)
    o_ref[...] = (jax.nn.sigmoid(ga_ref[...]) * pa + jax.nn.sigmoid(gs_ref[...]) * ps).astype(o_ref.dtype)


def gated_mix(ya, ys, wa, ws, gate_raw, tm=256):
    m, ka = ya.shape
    ks = ys.shape[1]
    n = wa.shape[1]
    row = lambda i: (i, 0)
    return pl.pallas_call(
        _gated_mix_kernel,
        grid=(m // tm,),
        in_specs=[pl.BlockSpec((tm, ka), row), pl.BlockSpec((tm, ks), row),
                  _resident((ka, n)), _resident((ks, n)),
                  pl.BlockSpec((tm, n), row), pl.BlockSpec((tm, n), lambda i: (i, 1))],
        out_specs=pl.BlockSpec((tm, n), row),
        out_shape=jax.ShapeDtypeStruct((m, n), BF16),
        compiler_params=_params(("parallel",)),
        name="gated_mix",
    )(ya, ys, wa, ws, gate_raw, gate_raw)


ATT_TQ = 128
ATT_TK = ATT_TQ + 2 * ATT_RADIUS
ATT_UNROLL = 4
ATT_SCORE_UNROLL = 8


def _attn_group(q_ref, k_ref, v_ref, cos_ref, sin_ref, qw_ref, kw_ref,
                qd_ref, kd_ref, vd_ref, bias_ref, acc_ref, m_ref, l_ref, *, s, dil, first):
    n = s // dil
    nblk = n // ATT_TQ
    seg = n + 2 * ATT_RADIUS
    scale = HEAD_DIM ** -0.5

    def rows_of(c):
        r, mb = c // nblk, c % nblk
        if dil == 1:
            return r, mb, pl.ds(pl.multiple_of(c * ATT_TQ, ATT_TQ), ATT_TQ)
        return r, mb, pl.ds(r + mb * (ATT_TQ * dil), ATT_TQ, stride=dil)

    def norm_rope(x, w_ref, c, out_scale):
        rows = pl.ds(pl.multiple_of(c * ATT_TQ, ATT_TQ), ATT_TQ)
        w, w_rolled = w_ref[0:1, :], w_ref[1:2, :]
        xr = pltpu.roll(x, HEAD_DIM // 2, 1)
        y = x * (cos_ref[rows, :] * w) + xr * (sin_ref[rows, :] * w_rolled)
        sq = x * x
        hi = sq.astype(BF16)
        lo = (sq - hi.astype(F32)).astype(BF16)
        ssq = jnp.dot(jnp.concatenate([hi, lo], axis=1), sum_lanes, preferred_element_type=F32)
        return y * (lax.rsqrt(ssq * (1.0 / HEAD_DIM) + RMS_EPS) * out_scale)

    sum_lanes = jnp.ones((2 * HEAD_DIM, HEAD_DIM), BF16)
    zeros = jnp.zeros((ATT_RADIUS, 2 * HEAD_DIM), BF16)
    ones = jnp.ones((ATT_TQ, HEAD_DIM), BF16)

    def zero_pads(r, carry):
        lo = pl.multiple_of(r * seg, ATT_RADIUS)
        hi = pl.multiple_of(r * seg + ATT_RADIUS + n, ATT_RADIUS)
        kd_ref[pl.ds(lo, ATT_RADIUS), :] = zeros[:, :HEAD_DIM]
        kd_ref[pl.ds(hi, ATT_RADIUS), :] = zeros[:, :HEAD_DIM]
        vd_ref[pl.ds(lo, ATT_RADIUS), :] = zeros
        vd_ref[pl.ds(hi, ATT_RADIUS), :] = zeros
        return carry

    lax.fori_loop(0, dil, zero_pads, 0)

    def prep(c2, carry):
        for u in range(ATT_UNROLL):
            c = c2 * ATT_UNROLL + u
            r, mb, rows = rows_of(c)
            qn = norm_rope(q_ref[rows, :], qw_ref, c, scale)
            kn = norm_rope(k_ref[rows, :], kw_ref, c, 1.0)
            qd_ref[pl.ds(pl.multiple_of(c * ATT_TQ, ATT_TQ), ATT_TQ), :] = qn.astype(BF16)
            dst = pl.ds(pl.multiple_of(r * seg + ATT_RADIUS + mb * ATT_TQ, ATT_RADIUS), ATT_TQ)
            kd_ref[dst, :] = kn.astype(BF16)
            vd_ref[dst, :HEAD_DIM] = v_ref[rows, :].astype(BF16)
            vd_ref[dst, HEAD_DIM:] = ones
        return carry

    lax.fori_loop(0, s // (ATT_TQ * ATT_UNROLL), prep, 0)

    def block(c2, carry):
        parts = []
        for u in range(ATT_SCORE_UNROLL):
            c = c2 * ATT_SCORE_UNROLL + u
            r, mb, rows = rows_of(c)
            qn = qd_ref[pl.ds(pl.multiple_of(c * ATT_TQ, ATT_TQ), ATT_TQ), :]
            win = pl.ds(pl.multiple_of(r * seg + mb * ATT_TQ, ATT_RADIUS), ATT_TK)
            edge = (mb == 0).astype(jnp.int32) + 2 * (mb == nblk - 1).astype(jnp.int32)
            sc = lax.dot_general(qn, kd_ref[win, :], (((1,), (1,)), ((), ())), preferred_element_type=F32)
            sc = sc + bias_ref[edge]
            mx = jnp.max(sc, axis=-1, keepdims=True)
            p = jnp.exp(sc - mx).astype(BF16)
            pv = jnp.dot(p, vd_ref[win, :], preferred_element_type=F32)
            parts.append((rows, mx, pv))
        for rows, mx, pv in parts:
            m_b = jnp.broadcast_to(mx, (ATT_TQ, HEAD_DIM))
            pv, l_b = pv[:, :HEAD_DIM], pv[:, HEAD_DIM:]
            if first:
                acc_n, m_n, l_n = pv, m_b, l_b
            else:
                m_o = m_ref[rows, :]
                m_n = jnp.maximum(m_o, m_b)
                a, b = jnp.exp(m_o - m_n), jnp.exp(m_b - m_n)
                acc_n = a * acc_ref[rows, :] + b * pv
                l_n = a * l_ref[rows, :] + b * l_b
            acc_ref[rows, :] = acc_n
            m_ref[rows, :] = m_n
            l_ref[rows, :] = l_n
        return carry

    lax.fori_loop(0, s // (ATT_TQ * ATT_SCORE_UNROLL), block, 0)


def _attn_kernel(q_ref, k_ref, v_ref, cos_ref, sin_ref, qw_ref, kw_ref, *rest, s):
    o_ref, qd_ref, kd_ref, vd_ref, bias_ref, acc_ref, m_ref, l_ref = rest[-8:]
    g = pl.program_id(2)
    ng = len(ATTN_GROUPS)

    qi = lax.broadcasted_iota(jnp.int32, (ATT_TQ, ATT_TK), 0)
    kj = lax.broadcasted_iota(jnp.int32, (ATT_TQ, ATT_TK), 1)
    band = jnp.abs(kj - ATT_RADIUS - qi) <= ATT_RADIUS
    for e in range(4):
        ok = band
        if e & 1:
            ok = ok & (kj >= ATT_RADIUS)
        if e & 2:
            ok = ok & (kj < ATT_TQ + ATT_RADIUS)
        bias_ref[e] = jnp.where(ok, 0.0, NEG_INF)

    for step, (_, dil) in enumerate(reversed(ATTN_GROUPS)):
        @pl.when(g == step)
        def _(dil=dil, step=step):
            _attn_group(q_ref, k_ref, v_ref, cos_ref, sin_ref, qw_ref, kw_ref,
                        qd_ref, kd_ref, vd_ref, bias_ref, acc_ref, m_ref, l_ref,
                        s=s, dil=dil, first=step == 0)

    @pl.when(g == ng - 1)
    def _():
        rows_per = 2 * ATT_TQ

        def finish(c, carry):
            rows = pl.ds(pl.multiple_of(c * rows_per, rows_per), rows_per)
            o_ref[rows, :] = (acc_ref[rows, :] / l_ref[rows, :]).astype(o_ref.dtype)
            return carry

        lax.fori_loop(0, s // rows_per, finish, 0)


def attention(qkv, cos, sin, q_norm, k_norm, row0, b, s, prev=None):
    shared = _SharedOut(prev)
    blk0 = row0 // s
    hpg = HEADS_PER_GROUP
    max_dil = max(d for _, d in ATTN_GROUPS)
    kv_rows = s + 2 * ATT_RADIUS * max_dil

    def col(base):
        return lambda bi, j, g: (blk0 + bi, base + (len(ATTN_GROUPS) - 1 - g) * hpg + j)

    def with_rolled(w):
        return jnp.stack([w, jnp.roll(w, HEAD_DIM // 2)])

    return pl.pallas_call(
        functools.partial(_attn_kernel, s=s),
        grid=(b, hpg, len(ATTN_GROUPS)),
        in_specs=[pl.BlockSpec((s, HEAD_DIM), col(0)),
                  pl.BlockSpec((s, HEAD_DIM), col(ATT_HEADS)),
                  pl.BlockSpec((s, HEAD_DIM), col(2 * ATT_HEADS)),
                  pl.BlockSpec((None, s, HEAD_DIM), lambda bi, j, g: (len(ATTN_GROUPS) - 1 - g, 0, 0)),
                  pl.BlockSpec((None, s, HEAD_DIM), lambda bi, j, g: (len(ATTN_GROUPS) - 1 - g, 0, 0)),
                  _resident((2, HEAD_DIM)),
                  _resident((2, HEAD_DIM))] + shared.in_specs,
        out_specs=pl.BlockSpec((s, HEAD_DIM), lambda bi, j, g: (blk0 + bi, j)),
        out_shape=jax.ShapeDtypeStruct((qkv.shape[0], ATT_OUT), BF16),
        input_output_aliases=shared.aliases(7),
        scratch_shapes=[pltpu.VMEM((s, HEAD_DIM), BF16),
                        pltpu.VMEM((kv_rows, HEAD_DIM), BF16),
                        pltpu.VMEM((kv_rows, 2 * HEAD_DIM), BF16),
                        pltpu.VMEM((4, ATT_TQ, ATT_TK), F32),
                        pltpu.VMEM((s, HEAD_DIM), F32),
                        pltpu.VMEM((s, HEAD_DIM), F32),
                        pltpu.VMEM((s, HEAD_DIM), F32)],
        compiler_params=_params(("parallel", "parallel", "arbitrary")),
        name=f"dilated_attention_s{s}",
    )(qkv, qkv, qkv, cos, sin, with_rolled(q_norm), with_rolled(k_norm), *shared.args)


def rope_tables(s):
    inv_freq = ROPE_THETA ** (-jnp.arange(0, HEAD_DIM, 2, dtype=F32) / HEAD_DIM)
    ang = jnp.arange(s, dtype=F32)[:, None] * inv_freq[None, :]
    cos, sin = jnp.cos(ang), jnp.sin(ang)
    cos, sin = jnp.concatenate([cos, cos], axis=-1), jnp.concatenate([-sin, sin], axis=-1)

    def by_subsequence(t):
        return jnp.stack([t.reshape(s // d, d, HEAD_DIM).swapaxes(0, 1).reshape(s, HEAD_DIM)
                          for _, d in ATTN_GROUPS])

    return by_subsequence(cos), by_subsequence(sin)


def _scan_rows(x, reverse):
    row = lax.broadcasted_iota(jnp.int32, x.shape, 0)
    sh = 1
    while sh < CHUNK:
        if reverse:
            x = x + jnp.where(row < CHUNK - sh, pltpu.roll(x, CHUNK - sh, 0), 0.0)
        else:
            x = x + jnp.where(row >= sh, pltpu.roll(x, sh, 0), 0.0)
        sh *= 2
    return x


def _ssd_fwd_kernel(xbc_ref, xprev_ref, xnext_ref, dt_ref, cw_ref, cb_ref, dtb_ref, alog_ref,
                    y_ref, xs_ref, bc_ref, state_ref, *, nc, di, gn):
    chunk = pl.program_id(1)
    x = xbc_ref[...]
    row = lax.broadcasted_iota(jnp.int32, (SUBLANES, 1), 0)
    prev_row = jnp.where(chunk > 0, xprev_ref[SUBLANES - 1:SUBLANES, :], 0.0)
    next_row = jnp.where(chunk < nc - 1, xnext_ref[0:1, :], 0.0)
    xm1 = pltpu.roll(x, 1, 0)
    xm1 = jnp.concatenate([jnp.where(row == 0, prev_row, xm1[:SUBLANES]), xm1[SUBLANES:]], axis=0)
    xp1 = pltpu.roll(x, CHUNK - 1, 0)
    xp1 = jnp.concatenate([xp1[:-SUBLANES], jnp.where(row == SUBLANES - 1, next_row, xp1[-SUBLANES:])], axis=0)
    xc = xm1 * cw_ref[0:1, :] + x * cw_ref[1:2, :] + xp1 * cw_ref[2:3, :] + cb_ref[...]
    xc = xc * jax.nn.sigmoid(xc)
    xs_ref[...] = xc[:, :di]
    bc_ref[...] = xc[:, di:].astype(BF16)
    _ssd_chunk(xs_ref, bc_ref, dt_ref, dtb_ref, alog_ref, y_ref, state_ref, di=di, gn=gn, reverse=False)


def _ssd_bwd_kernel(xs_ref, bc_ref, dt_ref, dtb_ref, alog_ref, yf_ref, z_ref, dskip_ref, nw_ref,
                    *rest, di, gn):
    o_ref, state_ref, y_ref = rest[-3:]
    _ssd_chunk(xs_ref, bc_ref, dt_ref, dtb_ref, alog_ref, y_ref, state_ref, di=di, gn=gn, reverse=True)
    y = yf_ref[...] + y_ref[...] + dskip_ref[...] * xs_ref[...]
    z = z_ref[...]
    y = y * (z * jax.nn.sigmoid(z))
    o_ref[...] = ((y * _row_rms(y)) * nw_ref[...]).astype(o_ref.dtype)


def _ssd_chunk(xs_ref, bc_ref, dt_ref, dtb_ref, alog_ref, y_ref, state_ref, *, di, gn, reverse):
    gw = di // SSM_GROUPS
    n_state = gn // SSM_GROUPS
    half = SSM_HEADDIM

    @pl.when(pl.program_id(1) == 0)
    def _():
        state_ref[...] = jnp.zeros_like(state_ref)

    dtr = dt_ref[...] + dtb_ref[...]
    dt = jnp.maximum(dtr, 0.0) + jnp.log1p(jnp.exp(-jnp.abs(dtr)))
    acum = _scan_rows(dt * (-jnp.exp(alog_ref[...])), reverse)
    total = acum[0:1, :] if reverse else acum[CHUNK - 1:CHUNK, :]
    wend = jnp.exp(total - acum) * dt
    src_t = (acum - jnp.log(dt)).T

    li = lax.broadcasted_iota(jnp.int32, (CHUNK, CHUNK), 0)
    si = lax.broadcasted_iota(jnp.int32, (CHUNK, CHUNK), 1)
    causal = (si >= li) if reverse else (li >= si)
    lane = lax.broadcasted_iota(jnp.int32, (CHUNK, LANES), 1)
    lo_half = lane < half
    lo_mask = lo_half.astype(BF16)
    hi_mask = 1 - lo_mask

    def lanes_of(col_vals, h):
        return jnp.broadcast_to(col_vals[:, h:h + 1], (CHUNK, LANES))

    for g in range(SSM_GROUPS):
        b_g = bc_ref[:, g * n_state:(g + 1) * n_state]
        c_g = bc_ref[:, gn + g * n_state:gn + (g + 1) * n_state]
        cb = lax.dot_general(c_g, b_g, (((1,), (1,)), ((), ())), preferred_element_type=F32).astype(BF16)
        st = state_ref[g]
        y_off = jnp.dot(c_g, st.astype(BF16), preferred_element_type=F32)
        wx_parts, dec_parts = [], []
        for pr in range(gw // LANES):
            col0 = g * gw + pr * LANES
            h0 = col0 // half
            xp = xs_ref[:, col0:col0 + LANES]
            sc, ecol, wcol = [], [], []
            for h in (h0, h0 + 1):
                a_col = lanes_of(acum, h)
                dec = jnp.exp(jnp.where(causal, a_col - src_t[h:h + 1, :], -jnp.inf))
                sc.append(cb * dec.astype(BF16))
                ecol.append(jnp.exp(a_col))
                wcol.append(lanes_of(wend, h))
            s2 = jnp.concatenate(sc, axis=1)
            xp16 = xp.astype(BF16)
            x2 = jnp.concatenate([xp16 * lo_mask, xp16 * hi_mask], axis=0)
            y = jnp.dot(s2, x2, preferred_element_type=F32)
            y = y + y_off[:, pr * LANES:(pr + 1) * LANES] * jnp.where(lo_half, ecol[0], ecol[1])
            y_ref[:, col0:col0 + LANES] = y
            wx_parts.append((jnp.where(lo_half, wcol[0], wcol[1]) * xp).astype(BF16))
            tot0 = jnp.broadcast_to(total[:, h0:h0 + 1], (1, LANES))
            tot1 = jnp.broadcast_to(total[:, h0 + 1:h0 + 2], (1, LANES))
            dec_parts.append(jnp.exp(jnp.where(lo_half[0:1, :], tot0, tot1)))
        wx = jnp.concatenate(wx_parts, axis=1) if len(wx_parts) > 1 else wx_parts[0]
        sdec = jnp.concatenate(dec_parts, axis=1) if len(dec_parts) > 1 else dec_parts[0]
        upd = lax.dot_general(b_g, wx, (((0,), (0,)), ((), ())), preferred_element_type=F32)
        state_ref[g] = st * sdec + upd


def ssd_mixer(xbc, dt_raw, z, conv_w, conv_b, dt_bias, a_log, d_skip_row, norm_w, row0, b, s, prev=None):
    shared = _SharedOut(prev)
    cd = conv_w.shape[1]
    heads = dt_bias.shape[-1]
    di = heads * SSM_HEADDIM
    gn = (cd - di) // 2
    nc = s // CHUNK
    cblk0 = row0 // CHUNK
    per8 = CHUNK // SUBLANES
    last8 = xbc.shape[0] // SUBLANES - 1
    pad = lambda v: jnp.zeros((1, LANES), F32).at[0, :heads].set(v)
    state = pltpu.VMEM((SSM_GROUPS, gn // SSM_GROUPS, di // SSM_GROUPS), F32)

    fwd_g = lambda bi, c: (cblk0 + bi * nc + c, 0)
    fwd_l = lambda bi, c: (bi * nc + c, 0)
    y_fwd, xs, bc = pl.pallas_call(
        functools.partial(_ssd_fwd_kernel, nc=nc, di=di, gn=gn),
        grid=(b, nc),
        in_specs=[pl.BlockSpec((CHUNK, cd), fwd_g),
                  pl.BlockSpec((SUBLANES, cd), lambda bi, c: (jnp.maximum(fwd_g(bi, c)[0] * per8 - 1, 0), 0)),
                  pl.BlockSpec((SUBLANES, cd), lambda bi, c: (jnp.minimum((fwd_g(bi, c)[0] + 1) * per8, last8), 0)),
                  pl.BlockSpec((CHUNK, LANES), fwd_g),
                  _resident((3, cd)), _resident((1, cd)), _resident((1, LANES)), _resident((1, LANES))],
        out_specs=[pl.BlockSpec((CHUNK, di), fwd_l), pl.BlockSpec((CHUNK, di), fwd_l),
                   pl.BlockSpec((CHUNK, 2 * gn), fwd_l)],
        out_shape=[jax.ShapeDtypeStruct((b * s, di), F32), jax.ShapeDtypeStruct((b * s, di), F32),
                   jax.ShapeDtypeStruct((b * s, 2 * gn), BF16)],
        scratch_shapes=[state],
        compiler_params=_params(("parallel", "arbitrary")),
        name=f"ssd_fwd_s{s}",
    )(xbc, xbc, xbc, dt_raw, conv_w, conv_b.reshape(1, cd), pad(dt_bias[0]), pad(a_log[0]))

    bwd_g = lambda bi, c: (cblk0 + bi * nc + nc - 1 - c, 0)
    bwd_l = lambda bi, c: (bi * nc + nc - 1 - c, 0)
    return pl.pallas_call(
        functools.partial(_ssd_bwd_kernel, di=di, gn=gn),
        grid=(b, nc),
        in_specs=[pl.BlockSpec((CHUNK, di), bwd_l), pl.BlockSpec((CHUNK, 2 * gn), bwd_l),
                  pl.BlockSpec((CHUNK, LANES), lambda bi, c: (bwd_g(bi, c)[0], 1)),
                  _resident((1, LANES)), _resident((1, LANES)),
                  pl.BlockSpec((CHUNK, di), bwd_l), pl.BlockSpec((CHUNK, di), bwd_g),
                  _resident((1, di)), _resident((1, di))] + shared.in_specs,
        out_specs=pl.BlockSpec((CHUNK, di), bwd_g),
        out_shape=jax.ShapeDtypeStruct((xbc.shape[0], di), BF16),
        input_output_aliases=shared.aliases(9),
        scratch_shapes=[state, pltpu.VMEM((CHUNK, di), F32)],
        compiler_params=_params(("parallel", "arbitrary")),
        name=f"ssd_bwd_s{s}",
    )(xs, bc, dt_raw, pad(dt_bias[1]), pad(a_log[1]), y_fwd, z, d_skip_row, norm_w.reshape(1, di), *shared.args)


def _derived_weights(p):
    heads = p["dt_bias"].shape[-1]
    di = p["ssm_norm"].shape[-1]
    cd = p["conv_w"].shape[-1]
    w_in = p["w_in"]
    depth, d, _ = w_in.shape
    o_dt = 3 * ATT_W + di + cd
    o_gate = o_dt + 2 * heads
    w_dt = jnp.zeros((depth, d, 2 * LANES), F32)
    w_dt = w_dt.at[:, :, :heads].set(w_in[:, :, o_dt:o_dt + heads])
    w_dt = w_dt.at[:, :, LANES:LANES + heads].set(w_in[:, :, o_dt + heads:o_gate])
    bf = lambda a: a.astype(BF16)
    return dict(
        ffn1_w_down=bf(p["ffn1_w_down"]), ffn2_w_down=bf(p["ffn2_w_down"]),
        w_dt=bf(w_dt), w_gate=bf(w_in[:, :, o_gate:]),
        w_attn_out=bf(p["w_attn_out"]), w_ssm_out=bf(p["w_ssm_out"]),
        d_skip_row=jnp.repeat(p["d_skip"], SSM_HEADDIM, axis=-1),
    )


def _layer(x, xw, r, p, c, layer, sets, ropes):
    di = p["ssm_norm"].shape[-1]
    cd = p["conv_w"].shape[-1]
    a = ffn_up(xw, r, p["ffn1_w_gate"], p["ffn1_w_up"], layer)
    x, xw, r = ffn_down(a, c["ffn1_w_down"][layer], x, 0.5, p["mix_norm"][layer])
    qkv = matmul_scaled(xw, r, p["w_in"], layer, 0, 3 * ATT_W, name="proj_qkv")
    z = matmul_scaled(xw, r, p["w_in"], layer, 3 * ATT_W, di, name="proj_z")
    xbc = matmul_scaled(xw, r, p["w_in"], layer, 3 * ATT_W + di, cd, name="proj_xbc")
    dt_raw = matmul_scaled(xw, r, c["w_dt"], layer, name="proj_dt")
    gate_raw = matmul_scaled(xw, r, c["w_gate"], layer, tn=1024, name="proj_gate")
    y_att = y_ssm = None
    for (row0, b, s), (cos, sin) in zip(sets, ropes):
        y_att = attention(qkv, cos, sin, p["q_norm"][layer], p["k_norm"][layer], row0, b, s, prev=y_att)
        y_ssm = ssd_mixer(xbc, dt_raw, z, p["conv_w"][layer], p["conv_b"][layer], p["dt_bias"][layer],
                          p["a_log"][layer], c["d_skip_row"][layer:layer + 1], p["ssm_norm"][layer],
                          row0, b, s, prev=y_ssm)
    mix = gated_mix(y_att, y_ssm, c["w_attn_out"][layer], c["w_ssm_out"][layer], gate_raw)
    x, xw, r = mix_out(mix, p["w_out"], layer, x, p["ffn2_norm"][layer])
    return x, ffn_up(xw, r, p["ffn2_w_gate"], p["ffn2_w_up"], layer)


def kernel(x_prompt, x_sample, ffn1_norm, ffn1_w_gate, ffn1_w_up, ffn1_w_down, mix_norm, w_in, q_norm, k_norm, conv_w, conv_b, dt_bias, a_log, d_skip, ssm_norm, w_attn_out, w_ssm_out, w_out, ffn2_norm, ffn2_w_gate, ffn2_w_up, ffn2_w_down):
    params = dict(ffn1_norm=ffn1_norm, ffn1_w_gate=ffn1_w_gate, ffn1_w_up=ffn1_w_up, ffn1_w_down=ffn1_w_down,
                  mix_norm=mix_norm, w_in=w_in, q_norm=q_norm, k_norm=k_norm, conv_w=conv_w, conv_b=conv_b,
                  dt_bias=dt_bias, a_log=a_log, d_skip=d_skip, ssm_norm=ssm_norm, w_attn_out=w_attn_out,
                  w_ssm_out=w_ssm_out, w_out=w_out, ffn2_norm=ffn2_norm, ffn2_w_gate=ffn2_w_gate,
                  ffn2_w_up=ffn2_w_up, ffn2_w_down=ffn2_w_down)
    d = x_prompt.shape[-1]
    bp, sp = x_prompt.shape[:2]
    bs, ss = x_sample.shape[:2]
    mp = bp * sp
    depth = ffn1_norm.shape[0]
    sets = ((0, bp, sp), (mp, bs, ss))
    ropes = (rope_tables(sp), rope_tables(ss))
    c = _derived_weights(params)
    x = jnp.concatenate([x_prompt.reshape(mp, d), x_sample.reshape(bs * ss, d)])
    xw, r = norm_prep(x, ffn1_norm[0])
    for layer in range(depth - 1):
        x, a = _layer(x, xw, r, params, c, layer, sets, ropes)
        x, xw, r = ffn_down(a, c["ffn2_w_down"][layer], x, 0.5, ffn1_norm[layer + 1])
    x, a = _layer(x, xw, r, params, c, depth - 1, sets, ropes)
    w_last = c["ffn2_w_down"][depth - 1]
    y_prompt = ffn_down(a, w_last, x, 0.5, row0=0, rows=mp)
    y_sample = ffn_down(a, w_last, x, 0.5, row0=mp, rows=bs * ss)
    return y_prompt.reshape(x_prompt.shape), y_sample.reshape(x_sample.shape)
```

```python
import functools

import jax
import jax.numpy as jnp
from jax import lax
from jax.experimental import pallas as pl
from jax.experimental.pallas import tpu as pltpu

HEAD_DIM = 128
ATTN_GROUPS = ((128, 1), (512, 4), (2048, 16))
HEADS_PER_GROUP = 4
ATT_HEADS = HEADS_PER_GROUP * len(ATTN_GROUPS)
ATT_W = ATT_HEADS * HEAD_DIM
ATT_OUT = HEADS_PER_GROUP * HEAD_DIM
ATT_RADIUS = 64
ROPE_THETA = 10000.0
SSM_GROUPS = 4
SSM_HEADDIM = 64
CHUNK = 128
RMS_EPS = 1e-6
NEG_INF = -1e30

LANES = 128
SUBLANES = 8
V7X_VMEM_LIMIT = 56 * 1024 * 1024

F32 = jnp.float32
BF16 = jnp.bfloat16

assert all(w // (2 * d) == ATT_RADIUS for w, d in ATTN_GROUPS)


def _params(semantics):
    return pltpu.CompilerParams(dimension_semantics=semantics, vmem_limit_bytes=V7X_VMEM_LIMIT)


def _tile(n, pref):
    t = min(pref, n)
    t -= t % LANES
    while n % t:
        t -= LANES
    return t


def _resident(shape):
    return pl.BlockSpec(shape, lambda *_: (0,) * len(shape), pipeline_mode=pl.Buffered(1))


class _SharedOut:
    def __init__(self, prev):
        self.args = [] if prev is None else [prev]
        self.in_specs = [pl.BlockSpec(memory_space=pl.ANY)] * len(self.args)

    def aliases(self, operand_index):
        return {operand_index: 0} if self.args else {}


def _row_rms(x):
    return lax.rsqrt(jnp.mean(x * x, axis=-1, keepdims=True) + RMS_EPS)


def _norm_prep_kernel(x_ref, w_ref, xw_ref, r_ref):
    x = x_ref[...]
    xw_ref[...] = (x * w_ref[...]).astype(BF16)
    r_ref[...] = jnp.broadcast_to(_row_rms(x), r_ref.shape)


def norm_prep(x, w, tm=256):
    m, d = x.shape
    return pl.pallas_call(
        _norm_prep_kernel,
        grid=(m // tm,),
        in_specs=[pl.BlockSpec((tm, d), lambda i: (i, 0)), _resident((1, d))],
        out_specs=[pl.BlockSpec((tm, d), lambda i: (i, 0)), pl.BlockSpec((tm, LANES), lambda i: (i, 0))],
        out_shape=[jax.ShapeDtypeStruct((m, d), BF16), jax.ShapeDtypeStruct((m, LANES), F32)],
        compiler_params=_params(("parallel",)),
        name="norm_prep",
    )(x, w.reshape(1, d))


def _layer_cols(w, layer, tn):
    return pl.BlockSpec((None, w.shape[1], tn), lambda i, j: (layer, 0, j))


def _mm_kernel(a_ref, r_ref, w_ref, o_ref):
    y = jnp.dot(a_ref[...], w_ref[...], preferred_element_type=F32)
    o_ref[...] = (y * r_ref[:, 0:1]).astype(o_ref.dtype)


def matmul_scaled(a, r, w, layer, tm=1024, tn=1024, name="matmul"):
    m, k = a.shape
    n = w.shape[2]
    tm, tn = min(tm, m), _tile(n, tn)
    return pl.pallas_call(
        _mm_kernel,
        grid=(m // tm, n // tn),
        in_specs=[pl.BlockSpec((tm, k), lambda i, j: (i, 0)),
                  pl.BlockSpec((tm, LANES), lambda i, j: (i, 0)),
                  _layer_cols(w, layer, tn)],
        out_specs=pl.BlockSpec((tm, tn), lambda i, j: (i, j)),
        out_shape=jax.ShapeDtypeStruct((m, n), F32),
        compiler_params=_params(("parallel", "arbitrary")),
        name=name,
    )(a, r, w)


def _ffn_up_kernel(h_ref, r_ref, wg_ref, wu_ref, o_ref):
    h = h_ref[...]
    r = r_ref[:, 0:1]
    g = jnp.dot(h, wg_ref[...], preferred_element_type=F32) * r
    u = jnp.dot(h, wu_ref[...], preferred_element_type=F32) * r
    o_ref[...] = (g * jax.nn.sigmoid(g) * u).astype(o_ref.dtype)


def ffn_up(h, r, wg, wu, layer, tm=1024, tn=512):
    m, k = h.shape
    n = wg.shape[2]
    tm, tn = min(tm, m), _tile(n, tn)
    return pl.pallas_call(
        _ffn_up_kernel,
        grid=(m // tm, n // tn),
        in_specs=[pl.BlockSpec((tm, k), lambda i, j: (i, 0)),
                  pl.BlockSpec((tm, LANES), lambda i, j: (i, 0)),
                  _layer_cols(wg, layer, tn),
                  _layer_cols(wu, layer, tn)],
        out_specs=pl.BlockSpec((tm, tn), lambda i, j: (i, j)),
        out_shape=jax.ShapeDtypeStruct((m, n), BF16),
        compiler_params=_params(("parallel", "arbitrary")),
        name="ffn_up",
    )(h, r, wg, wu)


def _ffn_down_kernel(a_ref, w_ref, x_ref, *rest, scale, emit_norm):
    xn = x_ref[...] + scale * jnp.dot(a_ref[...], w_ref[...], preferred_element_type=F32)
    if emit_norm:
        wn_ref, o_ref, xw_ref, r_ref = rest
        xw_ref[...] = (xn * wn_ref[...]).astype(BF16)
        r_ref[...] = jnp.broadcast_to(_row_rms(xn), r_ref.shape)
    else:
        (o_ref,) = rest
    o_ref[...] = xn


def ffn_down(a, w, x, scale, next_norm_w=None, row0=0, rows=None, tm=256):
    k = a.shape[1]
    m = a.shape[0] if rows is None else rows
    n = w.shape[1]
    emit = next_norm_w is not None
    row = lambda i: (i, 0)
    src_row = lambda i: (i + row0 // tm, 0)
    in_specs = [pl.BlockSpec((tm, k), src_row), _resident((k, n)), pl.BlockSpec((tm, n), src_row)]
    out_specs = [pl.BlockSpec((tm, n), row)]
    out_shape = [jax.ShapeDtypeStruct((m, n), F32)]
    args = [a, w, x]
    if emit:
        in_specs.append(_resident((1, n)))
        args.append(next_norm_w.reshape(1, n))
        out_specs += [pl.BlockSpec((tm, n), row), pl.BlockSpec((tm, LANES), row)]
        out_shape += [jax.ShapeDtypeStruct((m, n), BF16), jax.ShapeDtypeStruct((m, LANES), F32)]
    out = pl.pallas_call(
        functools.partial(_ffn_down_kernel, scale=scale, emit_norm=emit),
        grid=(m // tm,),
        in_specs=in_specs,
        out_specs=out_specs,
        out_shape=out_shape,
        compiler_params=_params(("parallel",)),
        name="ffn_down",
    )(*args)
    return out if emit else out[0]


def _mix_out_kernel(a_ref, w_ref, x_ref, wn_ref, o_ref, xw_ref, r_ref, ssq_ref, *, d):
    j = pl.program_id(1)
    xn = x_ref[...] + jnp.dot(a_ref[...], w_ref[...], preferred_element_type=F32)
    o_ref[...] = xn
    xw_ref[...] = (xn * wn_ref[...]).astype(BF16)
    part = jnp.broadcast_to(jnp.sum(xn * xn, axis=-1, keepdims=True), ssq_ref.shape)

    @pl.when(j == 0)
    def _():
        ssq_ref[...] = part

    @pl.when(j > 0)
    def _():
        ssq_ref[...] += part

    @pl.when(j == pl.num_programs(1) - 1)
    def _():
        r_ref[...] = lax.rsqrt(ssq_ref[...] / d + RMS_EPS)


def mix_out(a, w, layer, x, next_norm_w, tm=1024, tn=1024):
    m, k = a.shape
    n = w.shape[2]
    tm, tn = min(tm, m), _tile(n, tn)
    return pl.pallas_call(
        functools.partial(_mix_out_kernel, d=n),
        grid=(m // tm, n // tn),
        in_specs=[pl.BlockSpec((tm, k), lambda i, j: (i, 0), pipeline_mode=pl.Buffered(1)),
                  _layer_cols(w, layer, tn),
                  pl.BlockSpec((tm, tn), lambda i, j: (i, j)),
                  pl.BlockSpec((1, tn), lambda i, j: (0, j))],
        out_specs=[pl.BlockSpec((tm, tn), lambda i, j: (i, j)),
                   pl.BlockSpec((tm, tn), lambda i, j: (i, j)),
                   pl.BlockSpec((tm, LANES), lambda i, j: (i, 0))],
        out_shape=[jax.ShapeDtypeStruct((m, n), F32), jax.ShapeDtypeStruct((m, n), BF16),
                   jax.ShapeDtypeStruct((m, LANES), F32)],
        scratch_shapes=[pltpu.VMEM((tm, LANES), F32)],
        compiler_params=_params(("parallel", "arbitrary")),
        name="mix_out",
    )(a, w, x, next_norm_w.reshape(1, n))


def _gated_mix_kernel(ya_ref, ys_ref, wa_ref, ws_ref, ga_ref, gs_ref, o_ref):
    pa = jnp.dot(ya_ref[...], wa_ref[...], preferred_element_type=F32)
    ps = jnp.dot(ys_ref[...], ws_ref[...], preferred_element_type=F32)
    o_ref[...] = (jax.nn.sigmoid(ga_ref[...]) * pa + jax.nn.sigmoid(gs_ref[...]) * ps).astype(o_ref.dtype)


def gated_mix(ya, ys, wa, ws, gate_raw, tm=256):
    m, ka = ya.shape
    ks = ys.shape[1]
    n = wa.shape[1]
    row = lambda i: (i, 0)
    return pl.pallas_call(
        _gated_mix_kernel,
        grid=(m // tm,),
        in_specs=[pl.BlockSpec((tm, ka), row), pl.BlockSpec((tm, ks), row),
                  _resident((ka, n)), _resident((ks, n)),
                  pl.BlockSpec((tm, n), row), pl.BlockSpec((tm, n), lambda i: (i, 1))],
        out_specs=pl.BlockSpec((tm, n), row),
        out_shape=jax.ShapeDtypeStruct((m, n), BF16),
        compiler_params=_params(("parallel",)),
        name="gated_mix",
    )(ya, ys, wa, ws, gate_raw, gate_raw)


ATT_TQ = 128
ATT_TK = ATT_TQ + 2 * ATT_RADIUS
ATT_UNROLL = 4
ATT_SCORE_UNROLL = 8


def _attn_group(q_ref, k_ref, v_ref, cos_ref, sin_ref, qw_ref, kw_ref,
                qd_ref, kd_ref, vd_ref, bias_ref, acc_ref, m_ref, l_ref, *, s, dil, first):
    n = s // dil
    nblk = n // ATT_TQ
    seg = n + 2 * ATT_RADIUS
    scale = HEAD_DIM ** -0.5

    def rows_of(c):
        r, mb = c // nblk, c % nblk
        if dil == 1:
            return r, mb, pl.ds(pl.multiple_of(c * ATT_TQ, ATT_TQ), ATT_TQ)
        return r, mb, pl.ds(r + mb * (ATT_TQ * dil), ATT_TQ, stride=dil)

    def norm_rope(x, w_ref, c, out_scale):
        rows = pl.ds(pl.multiple_of(c * ATT_TQ, ATT_TQ), ATT_TQ)
        w, w_rolled = w_ref[0:1, :], w_ref[1:2, :]
        xr = pltpu.roll(x, HEAD_DIM // 2, 1)
        y = x * (cos_ref[rows, :] * w) + xr * (sin_ref[rows, :] * w_rolled)
        sq = x * x
        hi = sq.astype(BF16)
        lo = (sq - hi.astype(F32)).astype(BF16)
        ssq = jnp.dot(jnp.concatenate([hi, lo], axis=1), sum_lanes, preferred_element_type=F32)
        return y * (lax.rsqrt(ssq * (1.0 / HEAD_DIM) + RMS_EPS) * out_scale)

    sum_lanes = jnp.ones((2 * HEAD_DIM, HEAD_DIM), BF16)
    zeros = jnp.zeros((ATT_RADIUS, 2 * HEAD_DIM), BF16)
    ones = jnp.ones((ATT_TQ, HEAD_DIM), BF16)

    def zero_pads(r, carry):
        lo = pl.multiple_of(r * seg, ATT_RADIUS)
        hi = pl.multiple_of(r * seg + ATT_RADIUS + n, ATT_RADIUS)
        kd_ref[pl.ds(lo, ATT_RADIUS), :] = zeros[:, :HEAD_DIM]
        kd_ref[pl.ds(hi, ATT_RADIUS), :] = zeros[:, :HEAD_DIM]
        vd_ref[pl.ds(lo, ATT_RADIUS), :] = zeros
        vd_ref[pl.ds(hi, ATT_RADIUS), :] = zeros
        return carry

    lax.fori_loop(0, dil, zero_pads, 0)

    def prep(c2, carry):
        for u in range(ATT_UNROLL):
            c = c2 * ATT_UNROLL + u
            r, mb, rows = rows_of(c)
            qn = norm_rope(q_ref[rows, :], qw_ref, c, scale)
            kn = norm_rope(k_ref[rows, :], kw_ref, c, 1.0)
            qd_ref[pl.ds(pl.multiple_of(c * ATT_TQ, ATT_TQ), ATT_TQ), :] = qn.astype(BF16)
            dst = pl.ds(pl.multiple_of(r * seg + ATT_RADIUS + mb * ATT_TQ, ATT_RADIUS), ATT_TQ)
            kd_ref[dst, :] = kn.astype(BF16)
            vd_ref[dst, :HEAD_DIM] = v_ref[rows, :].astype(BF16)
            vd_ref[dst, HEAD_DIM:] = ones
        return carry

    lax.fori_loop(0, s // (ATT_TQ * ATT_UNROLL), prep, 0)

    def block(c2, carry):
        parts = []
        for u in range(ATT_SCORE_UNROLL):
            c = c2 * ATT_SCORE_UNROLL + u
            r, mb, rows = rows_of(c)
            qn = qd_ref[pl.ds(pl.multiple_of(c * ATT_TQ, ATT_TQ), ATT_TQ), :]
            win = pl.ds(pl.multiple_of(r * seg + mb * ATT_TQ, ATT_RADIUS), ATT_TK)
            edge = jnp.where(mb == 0, 1, 0) + jnp.where(mb == nblk - 1, 2, 0)
            sc = lax.dot_general(qn, kd_ref[win, :], (((1,), (1,)), ((), ())), preferred_element_type=F32)
            sc = sc + bias_ref[edge]
            mx = jnp.max(sc, axis=-1, keepdims=True)
            p = jnp.exp(sc - mx).astype(BF16)
            pv = jnp.dot(p, vd_ref[win, :], preferred_element_type=F32)
            parts.append((rows, mx, pv))
        for rows, mx, pv in parts:
            m_b = jnp.broadcast_to(mx, (ATT_TQ, HEAD_DIM))
            pv, l_b = pv[:, :HEAD_DIM], pv[:, HEAD_DIM:]
            if first:
                acc_n, m_n, l_n = pv, m_b, l_b
            else:
                m_o = m_ref[rows, :]
                m_n = jnp.maximum(m_o, m_b)
                a, b = jnp.exp(m_o - m_n), jnp.exp(m_b - m_n)
                acc_n = a * acc_ref[rows, :] + b * pv
                l_n = a * l_ref[rows, :] + b * l_b
            acc_ref[rows, :] = acc_n
            m_ref[rows, :] = m_n
            l_ref[rows, :] = l_n
        return carry

    lax.fori_loop(0, s // (ATT_TQ * ATT_SCORE_UNROLL), block, 0)


def _attn_kernel(q_ref, k_ref, v_ref, cos_ref, sin_ref, qw_ref, kw_ref, *rest, s):
    o_ref, qd_ref, kd_ref, vd_ref, bias_ref, acc_ref, m_ref, l_ref = rest[-8:]
    g = pl.program_id(2)
    ng = len(ATTN_GROUPS)

    qi = lax.broadcasted_iota(jnp.int32, (ATT_TQ, ATT_TK), 0)
    kj = lax.broadcasted_iota(jnp.int32, (ATT_TQ, ATT_TK), 1)
    band = jnp.abs(kj - ATT_RADIUS - qi) <= ATT_RADIUS
    for e in range(4):
        ok = band
        if e & 1:
            ok = ok & (kj >= ATT_RADIUS)
        if e & 2:
            ok = ok & (kj < ATT_TQ + ATT_RADIUS)
        bias_ref[e] = jnp.where(ok, 0.0, NEG_INF)

    for step, (_, dil) in enumerate(reversed(ATTN_GROUPS)):
        @pl.when(g == step)
        def _(dil=dil, step=step):
            _attn_group(q_ref, k_ref, v_ref, cos_ref, sin_ref, qw_ref, kw_ref,
                        qd_ref, kd_ref, vd_ref, bias_ref, acc_ref, m_ref, l_ref,
                        s=s, dil=dil, first=step == 0)

    @pl.when(g == ng - 1)
    def _():
        rows_per = 2 * ATT_TQ

        def finish(c, carry):
            rows = pl.ds(pl.multiple_of(c * rows_per, rows_per), rows_per)
            o_ref[rows, :] = (acc_ref[rows, :] / l_ref[rows, :]).astype(o_ref.dtype)
            return carry

        lax.fori_loop(0, s // rows_per, finish, 0)


def attention(qkv, cos, sin, q_norm, k_norm, row0, b, s, prev=None):
    shared = _SharedOut(prev)
    blk0 = row0 // s
    hpg = HEADS_PER_GROUP
    max_dil = max(d for _, d in ATTN_GROUPS)
    kv_rows = s + 2 * ATT_RADIUS * max_dil

    def col(base):
        return lambda bi, j, g: (blk0 + bi, base + (len(ATTN_GROUPS) - 1 - g) * hpg + j)

    def with_rolled(w):
        return jnp.stack([w, jnp.roll(w, HEAD_DIM // 2)])

    return pl.pallas_call(
        functools.partial(_attn_kernel, s=s),
        grid=(b, hpg, len(ATTN_GROUPS)),
        in_specs=[pl.BlockSpec((s, HEAD_DIM), col(0)),
                  pl.BlockSpec((s, HEAD_DIM), col(ATT_HEADS)),
                  pl.BlockSpec((s, HEAD_DIM), col(2 * ATT_HEADS)),
                  pl.BlockSpec((None, s, HEAD_DIM), lambda bi, j, g: (len(ATTN_GROUPS) - 1 - g, 0, 0)),
                  pl.BlockSpec((None, s, HEAD_DIM), lambda bi, j, g: (len(ATTN_GROUPS) - 1 - g, 0, 0)),
                  _resident((2, HEAD_DIM)),
                  _resident((2, HEAD_DIM))] + shared.in_specs,
        out_specs=pl.BlockSpec((s, HEAD_DIM), lambda bi, j, g: (blk0 + bi, j)),
        out_shape=jax.ShapeDtypeStruct((qkv.shape[0], ATT_OUT), BF16),
        input_output_aliases=shared.aliases(7),
        scratch_shapes=[pltpu.VMEM((s, HEAD_DIM), BF16),
                        pltpu.VMEM((kv_rows, HEAD_DIM), BF16),
                        pltpu.VMEM((kv_rows, 2 * HEAD_DIM), BF16),
                        pltpu.VMEM((4, ATT_TQ, ATT_TK), F32),
                        pltpu.VMEM((s, HEAD_DIM), F32),
                        pltpu.VMEM((s, HEAD_DIM), F32),
                        pltpu.VMEM((s, HEAD_DIM), F32)],
        compiler_params=_params(("parallel", "parallel", "arbitrary")),
        name=f"dilated_attention_s{s}",
    )(qkv, qkv, qkv, cos, sin, with_rolled(q_norm), with_rolled(k_norm), *shared.args)


def rope_tables(s):
    inv_freq = ROPE_THETA ** (-jnp.arange(0, HEAD_DIM, 2, dtype=F32) / HEAD_DIM)
    ang = jnp.arange(s, dtype=F32)[:, None] * inv_freq[None, :]
    cos, sin = jnp.cos(ang), jnp.sin(ang)
    cos, sin = jnp.concatenate([cos, cos], axis=-1), jnp.concatenate([-sin, sin], axis=-1)

    def by_subsequence(t):
        return jnp.stack([t.reshape(s // d, d, HEAD_DIM).swapaxes(0, 1).reshape(s, HEAD_DIM)
                          for _, d in ATTN_GROUPS])

    return by_subsequence(cos), by_subsequence(sin)


def _scan_rows(x, reverse):
    row = lax.broadcasted_iota(jnp.int32, x.shape, 0)
    sh = 1
    while sh < CHUNK:
        if reverse:
            x = x + jnp.where(row < CHUNK - sh, pltpu.roll(x, CHUNK - sh, 0), 0.0)
        else:
            x = x + jnp.where(row >= sh, pltpu.roll(x, sh, 0), 0.0)
        sh *= 2
    return x


def _ssd_fwd_kernel(xbc_ref, xprev_ref, xnext_ref, dt_ref, cw_ref, cb_ref, dtb_ref, alog_ref,
                    y_ref, xs_ref, bc_ref, state_ref, *, nc, di, gn):
    chunk = pl.program_id(1)
    x = xbc_ref[...]
    row = lax.broadcasted_iota(jnp.int32, (SUBLANES, 1), 0)
    prev_row = jnp.where(chunk > 0, xprev_ref[SUBLANES - 1:SUBLANES, :], 0.0)
    next_row = jnp.where(chunk < nc - 1, xnext_ref[0:1, :], 0.0)
    xm1 = pltpu.roll(x, 1, 0)
    xm1 = jnp.concatenate([jnp.where(row == 0, prev_row, xm1[:SUBLANES]), xm1[SUBLANES:]], axis=0)
    xp1 = pltpu.roll(x, CHUNK - 1, 0)
    xp1 = jnp.concatenate([xp1[:-SUBLANES], jnp.where(row == SUBLANES - 1, next_row, xp1[-SUBLANES:])], axis=0)
    xc = xm1 * cw_ref[0:1, :] + x * cw_ref[1:2, :] + xp1 * cw_ref[2:3, :] + cb_ref[...]
    xc = xc * jax.nn.sigmoid(xc)
    xs_ref[...] = xc[:, :di]
    bc_ref[...] = xc[:, di:].astype(BF16)
    _ssd_chunk(xs_ref, bc_ref, dt_ref, dtb_ref, alog_ref, y_ref, state_ref, di=di, gn=gn, reverse=False)


def _ssd_bwd_kernel(xs_ref, bc_ref, dt_ref, dtb_ref, alog_ref, yf_ref, z_ref, dskip_ref, nw_ref,
                    *rest, di, gn):
    o_ref, state_ref, y_ref = rest[-3:]
    _ssd_chunk(xs_ref, bc_ref, dt_ref, dtb_ref, alog_ref, y_ref, state_ref, di=di, gn=gn, reverse=True)
    y = yf_ref[...] + y_ref[...] + dskip_ref[...] * xs_ref[...]
    z = z_ref[...]
    y = y * (z * jax.nn.sigmoid(z))
    o_ref[...] = ((y * _row_rms(y)) * nw_ref[...]).astype(o_ref.dtype)


def _ssd_chunk(xs_ref, bc_ref, dt_ref, dtb_ref, alog_ref, y_ref, state_ref, *, di, gn, reverse):
    gw = di // SSM_GROUPS
    n_state = gn // SSM_GROUPS
    half = SSM_HEADDIM

    @pl.when(pl.program_id(1) == 0)
    def _():
        state_ref[...] = jnp.zeros_like(state_ref)

    dtr = dt_ref[...] + dtb_ref[...]
    dt = jnp.maximum(dtr, 0.0) + jnp.log1p(jnp.exp(-jnp.abs(dtr)))
    acum = _scan_rows(dt * (-jnp.exp(alog_ref[...])), reverse)
    total = acum[0:1, :] if reverse else acum[CHUNK - 1:CHUNK, :]
    wend = jnp.exp(total - acum) * dt
    src_t = (acum - jnp.log(dt)).T

    li = lax.broadcasted_iota(jnp.int32, (CHUNK, CHUNK), 0)
    si = lax.broadcasted_iota(jnp.int32, (CHUNK, CHUNK), 1)
    causal = (si >= li) if reverse else (li >= si)
    lane = lax.broadcasted_iota(jnp.int32, (CHUNK, LANES), 1)
    lo_half = lane < half
    lo_mask = lo_half.astype(BF16)
    hi_mask = 1 - lo_mask

    def lanes_of(col_vals, h):
        return jnp.broadcast_to(col_vals[:, h:h + 1], (CHUNK, LANES))

    for g in range(SSM_GROUPS):
        b_g = bc_ref[:, g * n_state:(g + 1) * n_state]
        c_g = bc_ref[:, gn + g * n_state:gn + (g + 1) * n_state]
        cb = lax.dot_general(c_g, b_g, (((1,), (1,)), ((), ())), preferred_element_type=F32).astype(BF16)
        st = state_ref[g]
        y_off = jnp.dot(c_g, st.astype(BF16), preferred_element_type=F32)
        wx_parts, dec_parts = [], []
        for pr in range(gw // LANES):
            col0 = g * gw + pr * LANES
            h0 = col0 // half
            xp = xs_ref[:, col0:col0 + LANES]
            sc, ecol, wcol = [], [], []
            for h in (h0, h0 + 1):
                a_col = lanes_of(acum, h)
                dec = jnp.exp(jnp.where(causal, a_col - src_t[h:h + 1, :], -jnp.inf))
                sc.append(cb * dec.astype(BF16))
                ecol.append(jnp.exp(a_col))
                wcol.append(lanes_of(wend, h))
            s2 = jnp.concatenate(sc, axis=1)
            xp16 = xp.astype(BF16)
            x2 = jnp.concatenate([xp16 * lo_mask, xp16 * hi_mask], axis=0)
            y = jnp.dot(s2, x2, preferred_element_type=F32)
            y = y + y_off[:, pr * LANES:(pr + 1) * LANES] * jnp.where(lo_half, ecol[0], ecol[1])
            y_ref[:, col0:col0 + LANES] = y
            wx_parts.append((jnp.where(lo_half, wcol[0], wcol[1]) * xp).astype(BF16))
            tot0 = jnp.broadcast_to(total[:, h0:h0 + 1], (1, LANES))
            tot1 = jnp.broadcast_to(total[:, h0 + 1:h0 + 2], (1, LANES))
            dec_parts.append(jnp.exp(jnp.where(lo_half[0:1, :], tot0, tot1)))
        wx = jnp.concatenate(wx_parts, axis=1) if len(wx_parts) > 1 else wx_parts[0]
        sdec = jnp.concatenate(dec_parts, axis=1) if len(dec_parts) > 1 else dec_parts[0]
        upd = lax.dot_general(b_g, wx, (((0,), (0,)), ((), ())), preferred_element_type=F32)
        state_ref[g] = st * sdec + upd


def ssd_mixer(xbc, z_dt, conv_w, conv_b, dt_bias, a_log, d_skip_row, norm_w, row0, b, s, prev=None):
    shared = _SharedOut(prev)
    cd = conv_w.shape[1]
    heads = dt_bias.shape[-1]
    di = heads * SSM_HEADDIM
    gn = (cd - di) // 2
    nc = s // CHUNK
    cblk0 = row0 // CHUNK
    per8 = CHUNK // SUBLANES
    last8 = xbc.shape[0] // SUBLANES - 1
    pad = lambda v: jnp.zeros((1, LANES), F32).at[0, :heads].set(v)
    state = pltpu.VMEM((SSM_GROUPS, gn // SSM_GROUPS, di // SSM_GROUPS), F32)

    fwd_g = lambda bi, c: (cblk0 + bi * nc + c, 0)
    fwd_l = lambda bi, c: (bi * nc + c, 0)
    y_fwd, xs, bc = pl.pallas_call(
        functools.partial(_ssd_fwd_kernel, nc=nc, di=di, gn=gn),
        grid=(b, nc),
        in_specs=[pl.BlockSpec((CHUNK, cd), fwd_g),
                  pl.BlockSpec((SUBLANES, cd), lambda bi, c: (jnp.maximum(fwd_g(bi, c)[0] * per8 - 1, 0), 0)),
                  pl.BlockSpec((SUBLANES, cd), lambda bi, c: (jnp.minimum((fwd_g(bi, c)[0] + 1) * per8, last8), 0)),
                  pl.BlockSpec((CHUNK, LANES), lambda bi, c: (fwd_g(bi, c)[0], di // LANES)),
                  _resident((3, cd)), _resident((1, cd)), _resident((1, LANES)), _resident((1, LANES))],
        out_specs=[pl.BlockSpec((CHUNK, di), fwd_l), pl.BlockSpec((CHUNK, di), fwd_l),
                   pl.BlockSpec((CHUNK, 2 * gn), fwd_l)],
        out_shape=[jax.ShapeDtypeStruct((b * s, di), F32), jax.ShapeDtypeStruct((b * s, di), F32),
                   jax.ShapeDtypeStruct((b * s, 2 * gn), BF16)],
        scratch_shapes=[state],
        compiler_params=_params(("parallel", "arbitrary")),
        name=f"ssd_fwd_s{s}",
    )(xbc, xbc, xbc, z_dt, conv_w, conv_b.reshape(1, cd), pad(dt_bias[0]), pad(a_log[0]))

    bwd_g = lambda bi, c: (cblk0 + bi * nc + nc - 1 - c, 0)
    bwd_l = lambda bi, c: (bi * nc + nc - 1 - c, 0)
    return pl.pallas_call(
        functools.partial(_ssd_bwd_kernel, di=di, gn=gn),
        grid=(b, nc),
        in_specs=[pl.BlockSpec((CHUNK, di), bwd_l), pl.BlockSpec((CHUNK, 2 * gn), bwd_l),
                  pl.BlockSpec((CHUNK, LANES), lambda bi, c: (bwd_g(bi, c)[0], di // LANES + 1)),
                  _resident((1, LANES)), _resident((1, LANES)),
                  pl.BlockSpec((CHUNK, di), bwd_l), pl.BlockSpec((CHUNK, di), bwd_g),
                  _resident((1, di)), _resident((1, di))] + shared.in_specs,
        out_specs=pl.BlockSpec((CHUNK, di), bwd_g),
        out_shape=jax.ShapeDtypeStruct((xbc.shape[0], di), BF16),
        input_output_aliases=shared.aliases(9),
        scratch_shapes=[state, pltpu.VMEM((CHUNK, di), F32)],
        compiler_params=_params(("parallel", "arbitrary")),
        name=f"ssd_bwd_s{s}",
    )(xs, bc, z_dt, pad(dt_bias[1]), pad(a_log[1]), y_fwd, z_dt, d_skip_row, norm_w.reshape(1, di), *shared.args)


def _derived_weights(p):
    heads = p["dt_bias"].shape[-1]
    di = p["ssm_norm"].shape[-1]
    cd = p["conv_w"].shape[-1]
    w_in = p["w_in"]
    depth, d, _ = w_in.shape
    o_z = 3 * ATT_W
    o_xbc = o_z + di
    o_dt = o_xbc + cd
    o_gate = o_dt + 2 * heads
    w_dt = jnp.zeros((depth, d, 2 * LANES), F32)
    w_dt = w_dt.at[:, :, :heads].set(w_in[:, :, o_dt:o_dt + heads])
    w_dt = w_dt.at[:, :, LANES:LANES + heads].set(w_in[:, :, o_dt + heads:o_gate])
    bf = lambda a: a.astype(BF16)
    return dict(
        ffn1_w_gate=bf(p["ffn1_w_gate"]), ffn1_w_up=bf(p["ffn1_w_up"]), ffn1_w_down=bf(p["ffn1_w_down"]),
        ffn2_w_gate=bf(p["ffn2_w_gate"]), ffn2_w_up=bf(p["ffn2_w_up"]), ffn2_w_down=bf(p["ffn2_w_down"]),
        w_qkv=bf(w_in[:, :, :o_z]), w_xbc=bf(w_in[:, :, o_xbc:o_dt]),
        w_z_dt=bf(jnp.concatenate([w_in[:, :, o_z:o_xbc], w_dt], axis=-1)), w_gate=bf(w_in[:, :, o_gate:]),
        w_attn_out=bf(p["w_attn_out"]), w_ssm_out=bf(p["w_ssm_out"]), w_out=bf(p["w_out"]),
        d_skip_row=jnp.repeat(p["d_skip"], SSM_HEADDIM, axis=-1),
    )


def _layer(x, xw, r, p, c, layer, sets, ropes):
    a = ffn_up(xw, r, c["ffn1_w_gate"], c["ffn1_w_up"], layer)
    x, xw, r = ffn_down(a, c["ffn1_w_down"][layer], x, 0.5, p["mix_norm"][layer])
    qkv = matmul_scaled(xw, r, c["w_qkv"], layer, tn=768, name="proj_qkv")
    z_dt = matmul_scaled(xw, r, c["w_z_dt"], layer, tn=768, name="proj_z_dt")
    xbc = matmul_scaled(xw, r, c["w_xbc"], layer, name="proj_xbc")
    gate_raw = matmul_scaled(xw, r, c["w_gate"], layer, name="proj_gate")
    y_att = y_ssm = None
    for (row0, b, s), (cos, sin) in zip(sets, ropes):
        y_att = attention(qkv, cos, sin, p["q_norm"][layer], p["k_norm"][layer], row0, b, s, prev=y_att)
        y_ssm = ssd_mixer(xbc, z_dt, p["conv_w"][layer], p["conv_b"][layer], p["dt_bias"][layer],
                          p["a_log"][layer], c["d_skip_row"][layer:layer + 1], p["ssm_norm"][layer],
                          row0, b, s, prev=y_ssm)
    mix = gated_mix(y_att, y_ssm, c["w_attn_out"][layer], c["w_ssm_out"][layer], gate_raw)
    x, xw, r = mix_out(mix, c["w_out"], layer, x, p["ffn2_norm"][layer])
    return x, ffn_up(xw, r, c["ffn2_w_gate"], c["ffn2_w_up"], layer)


def kernel(x_prompt, x_sample, ffn1_norm, ffn1_w_gate, ffn1_w_up, ffn1_w_down, mix_norm, w_in, q_norm, k_norm, conv_w, conv_b, dt_bias, a_log, d_skip, ssm_norm, w_attn_out, w_ssm_out, w_out, ffn2_norm, ffn2_w_gate, ffn2_w_up, ffn2_w_down):
    params = dict(ffn1_norm=ffn1_norm, ffn1_w_gate=ffn1_w_gate, ffn1_w_up=ffn1_w_up, ffn1_w_down=ffn1_w_down,
                  mix_norm=mix_norm, w_in=w_in, q_norm=q_norm, k_norm=k_norm, conv_w=conv_w, conv_b=conv_b,
                  dt_bias=dt_bias, a_log=a_log, d_skip=d_skip, ssm_norm=ssm_norm, w_attn_out=w_attn_out,
                  w_ssm_out=w_ssm_out, w_out=w_out, ffn2_norm=ffn2_norm, ffn2_w_gate=ffn2_w_gate,
                  ffn2_w_up=ffn2_w_up, ffn2_w_down=ffn2_w_down)
    d = x_prompt.shape[-1]
    bp, sp = x_prompt.shape[:2]
    bs, ss = x_sample.shape[:2]
    mp = bp * sp
    depth = ffn1_norm.shape[0]
    sets = ((0, bp, sp), (mp, bs, ss))
    ropes = (rope_tables(sp), rope_tables(ss))
    c = _derived_weights(params)
    x = jnp.concatenate([x_prompt.reshape(mp, d), x_sample.reshape(bs * ss, d)])
    xw, r = norm_prep(x, ffn1_norm[0])
    for layer in range(depth - 1):
        x, a = _layer(x, xw, r, params, c, layer, sets, ropes)
        x, xw, r = ffn_down(a, c["ffn2_w_down"][layer], x, 0.5, ffn1_norm[layer + 1])
    x, a = _layer(x, xw, r, params, c, depth - 1, sets, ropes)
    w_last = c["ffn2_w_down"][depth - 1]
    y_prompt = ffn_down(a, w_last, x, 0.5, row0=0, rows=mp)
    y_sample = ffn_down(a, w_last, x, 0.5, row0=mp, rows=bs * ss)
    return y_prompt.reshape(x_prompt.shape), y_sample.reshape(x_sample.shape)
```

```python
import functools

import jax
import jax.numpy as jnp
from jax import lax
from jax.experimental import pallas as pl
from jax.experimental.pallas import tpu as pltpu

HEAD_DIM = 128
ATTN_GROUPS = ((128, 1), (512, 4), (2048, 16))
HEADS_PER_GROUP = 4
ATT_HEADS = HEADS_PER_GROUP * len(ATTN_GROUPS)
ATT_W = ATT_HEADS * HEAD_DIM
ATT_OUT = HEADS_PER_GROUP * HEAD_DIM
ATT_RADIUS = 64
ROPE_THETA = 10000.0
SSM_GROUPS = 4
SSM_HEADDIM = 64
CHUNK = 128
RMS_EPS = 1e-6
NEG_INF = -1e30

LANES = 128
SUBLANES = 8
V7X_VMEM_LIMIT = 56 * 1024 * 1024

F32 = jnp.float32
BF16 = jnp.bfloat16

assert all(w // (2 * d) == ATT_RADIUS for w, d in ATTN_GROUPS)


def _params(semantics):
    return pltpu.CompilerParams(dimension_semantics=semantics, vmem_limit_bytes=V7X_VMEM_LIMIT)


def _tile(n, pref):
    t = min(pref, n)
    t -= t % LANES
    while n % t:
        t -= LANES
    return t


def _resident(shape):
    return pl.BlockSpec(shape, lambda *_: (0,) * len(shape), pipeline_mode=pl.Buffered(1))


class _SharedOut:
    def __init__(self, prev):
        self.args = [] if prev is None else [prev]
        self.in_specs = [pl.BlockSpec(memory_space=pl.ANY)] * len(self.args)

    def aliases(self, operand_index):
        return {operand_index: 0} if self.args else {}


def _resident_layer(w, layer):
    return pl.BlockSpec((None,) + w.shape[1:], lambda *_: (layer, 0, 0), pipeline_mode=pl.Buffered(1))


def _row_rms(x):
    return lax.rsqrt(jnp.mean(x * x, axis=-1, keepdims=True) + RMS_EPS)


def _norm_prep_kernel(x_ref, w_ref, xw_ref, r_ref):
    x = x_ref[...]
    xw_ref[...] = (x * w_ref[...]).astype(BF16)
    r_ref[...] = jnp.broadcast_to(_row_rms(x), r_ref.shape)


def norm_prep(x, w, tm=256):
    m, d = x.shape
    return pl.pallas_call(
        _norm_prep_kernel,
        grid=(m // tm,),
        in_specs=[pl.BlockSpec((tm, d), lambda i: (i, 0)), _resident((1, d))],
        out_specs=[pl.BlockSpec((tm, d), lambda i: (i, 0)), pl.BlockSpec((tm, LANES), lambda i: (i, 0))],
        out_shape=[jax.ShapeDtypeStruct((m, d), BF16), jax.ShapeDtypeStruct((m, LANES), F32)],
        compiler_params=_params(("parallel",)),
        name="norm_prep",
    )(x, w.reshape(1, d))


def _layer_cols(w, layer, tn):
    return pl.BlockSpec((None, w.shape[1], tn), lambda i, j: (layer, 0, j))


def _mm_kernel(a_ref, r_ref, w_ref, o_ref):
    y = jnp.dot(a_ref[...], w_ref[...], preferred_element_type=F32)
    o_ref[...] = (y * r_ref[:, 0:1]).astype(o_ref.dtype)


def matmul_scaled(a, r, w, layer, tm=1024, tn=1024, name="matmul"):
    m, k = a.shape
    n = w.shape[2]
    tm, tn = min(tm, m), _tile(n, tn)
    return pl.pallas_call(
        _mm_kernel,
        grid=(m // tm, n // tn),
        in_specs=[pl.BlockSpec((tm, k), lambda i, j: (i, 0)),
                  pl.BlockSpec((tm, LANES), lambda i, j: (i, 0)),
                  _layer_cols(w, layer, tn)],
        out_specs=pl.BlockSpec((tm, tn), lambda i, j: (i, j)),
        out_shape=jax.ShapeDtypeStruct((m, n), F32),
        compiler_params=_params(("parallel", "arbitrary")),
        name=name,
    )(a, r, w)


def _ffn_up_kernel(h_ref, r_ref, wg_ref, wu_ref, o_ref):
    h = h_ref[...]
    r = r_ref[:, 0:1]
    g = jnp.dot(h, wg_ref[...], preferred_element_type=F32) * r
    u = jnp.dot(h, wu_ref[...], preferred_element_type=F32) * r
    o_ref[...] = (g * jax.nn.sigmoid(g) * u).astype(o_ref.dtype)


def ffn_up(h, r, wg, wu, layer, tm=1024, tn=512):
    m, k = h.shape
    n = wg.shape[2]
    tm, tn = min(tm, m), _tile(n, tn)
    return pl.pallas_call(
        _ffn_up_kernel,
        grid=(m // tm, n // tn),
        in_specs=[pl.BlockSpec((tm, k), lambda i, j: (i, 0)),
                  pl.BlockSpec((tm, LANES), lambda i, j: (i, 0)),
                  _layer_cols(wg, layer, tn),
                  _layer_cols(wu, layer, tn)],
        out_specs=pl.BlockSpec((tm, tn), lambda i, j: (i, j)),
        out_shape=jax.ShapeDtypeStruct((m, n), BF16),
        compiler_params=_params(("parallel", "arbitrary")),
        name="ffn_up",
    )(h, r, wg, wu)


def _ffn_down_kernel(a_ref, w_ref, x_ref, *rest, scale, emit_norm):
    xn = x_ref[...] + scale * jnp.dot(a_ref[...], w_ref[...], preferred_element_type=F32)
    if emit_norm:
        wn_ref, o_ref, xw_ref, r_ref = rest
        xw_ref[...] = (xn * wn_ref[...]).astype(BF16)
        r_ref[...] = jnp.broadcast_to(_row_rms(xn), r_ref.shape)
    else:
        (o_ref,) = rest
    o_ref[...] = xn


def ffn_down(a, w, layer, x, scale, next_norm_w=None, row0=0, rows=None, tm=256):
    k = a.shape[1]
    m = a.shape[0] if rows is None else rows
    n = w.shape[2]
    emit = next_norm_w is not None
    row = lambda i: (i, 0)
    src_row = lambda i: (i + row0 // tm, 0)
    in_specs = [pl.BlockSpec((tm, k), src_row), _resident_layer(w, layer), pl.BlockSpec((tm, n), src_row)]
    out_specs = [pl.BlockSpec((tm, n), row)]
    out_shape = [jax.ShapeDtypeStruct((m, n), F32)]
    args = [a, w, x]
    if emit:
        in_specs.append(_resident((1, n)))
        args.append(next_norm_w.reshape(1, n))
        out_specs += [pl.BlockSpec((tm, n), row), pl.BlockSpec((tm, LANES), row)]
        out_shape += [jax.ShapeDtypeStruct((m, n), BF16), jax.ShapeDtypeStruct((m, LANES), F32)]
    out = pl.pallas_call(
        functools.partial(_ffn_down_kernel, scale=scale, emit_norm=emit),
        grid=(m // tm,),
        in_specs=in_specs,
        out_specs=out_specs,
        out_shape=out_shape,
        compiler_params=_params(("parallel",)),
        name="ffn_down",
    )(*args)
    return out if emit else out[0]


def _mix_out_kernel(a_ref, w_ref, x_ref, wn_ref, o_ref, xw_ref, r_ref, ssq_ref, *, d):
    j = pl.program_id(1)
    xn = x_ref[...] + jnp.dot(a_ref[...], w_ref[...], preferred_element_type=F32)
    o_ref[...] = xn
    xw_ref[...] = (xn * wn_ref[...]).astype(BF16)
    part = jnp.broadcast_to(jnp.sum(xn * xn, axis=-1, keepdims=True), ssq_ref.shape)

    @pl.when(j == 0)
    def _():
        ssq_ref[...] = part

    @pl.when(j > 0)
    def _():
        ssq_ref[...] += part

    @pl.when(j == pl.num_programs(1) - 1)
    def _():
        r_ref[...] = lax.rsqrt(ssq_ref[...] / d + RMS_EPS)


def mix_out(a, w, layer, x, next_norm_w, tm=1024, tn=512):
    m, k = a.shape
    n = w.shape[2]
    tm, tn = min(tm, m), _tile(n, tn)
    return pl.pallas_call(
        functools.partial(_mix_out_kernel, d=n),
        grid=(m // tm, n // tn),
        in_specs=[pl.BlockSpec((tm, k), lambda i, j: (i, 0)),
                  _layer_cols(w, layer, tn),
                  pl.BlockSpec((tm, tn), lambda i, j: (i, j)),
                  pl.BlockSpec((1, tn), lambda i, j: (0, j))],
        out_specs=[pl.BlockSpec((tm, tn), lambda i, j: (i, j)),
                   pl.BlockSpec((tm, tn), lambda i, j: (i, j)),
                   pl.BlockSpec((tm, LANES), lambda i, j: (i, 0))],
        out_shape=[jax.ShapeDtypeStruct((m, n), F32), jax.ShapeDtypeStruct((m, n), BF16),
                   jax.ShapeDtypeStruct((m, LANES), F32)],
        scratch_shapes=[pltpu.VMEM((tm, LANES), F32)],
        compiler_params=_params(("parallel", "arbitrary")),
        name="mix_out",
    )(a, w, x, next_norm_w.reshape(1, n))


def _gated_mix_kernel(ya_ref, ys_ref, wa_ref, ws_ref, ga_ref, gs_ref, o_ref):
    pa = jnp.dot(ya_ref[...], wa_ref[...], preferred_element_type=F32)
    ps = jnp.dot(ys_ref[...], ws_ref[...], preferred_element_type=F32)
    o_ref[...] = (jax.nn.sigmoid(ga_ref[...]) * pa + jax.nn.sigmoid(gs_ref[...]) * ps).astype(o_ref.dtype)


def gated_mix(ya, ys, wa, ws, layer, gate_raw, tm=256):
    m, ka = ya.shape
    ks = ys.shape[1]
    n = wa.shape[2]
    row = lambda i: (i, 0)
    return pl.pallas_call(
        _gated_mix_kernel,
        grid=(m // tm,),
        in_specs=[pl.BlockSpec((tm, ka), row), pl.BlockSpec((tm, ks), row),
                  _resident_layer(wa, layer), _resident_layer(ws, layer),
                  pl.BlockSpec((tm, n), row), pl.BlockSpec((tm, n), lambda i: (i, 1))],
        out_specs=pl.BlockSpec((tm, n), row),
        out_shape=jax.ShapeDtypeStruct((m, n), BF16),
        compiler_params=_params(("parallel",)),
        name="gated_mix",
    )(ya, ys, wa, ws, gate_raw, gate_raw)


ATT_TQ = 128
ATT_TK = ATT_TQ + 2 * ATT_RADIUS
ATT_UNROLL = 4
ATT_SCORE_UNROLL = 8


def _attn_group(q_ref, k_ref, v_ref, cos_ref, sin_ref, qw_ref, kw_ref,
                qd_ref, kd_ref, vd_ref, bias_ref, acc_ref, m_ref, l_ref, *, s, dil, first):
    n = s // dil
    nblk = n // ATT_TQ
    seg = n + 2 * ATT_RADIUS
    scale = HEAD_DIM ** -0.5

    def rows_of(c):
        r, mb = c // nblk, c % nblk
        if dil == 1:
            return r, mb, pl.ds(pl.multiple_of(c * ATT_TQ, ATT_TQ), ATT_TQ)
        return r, mb, pl.ds(r + mb * (ATT_TQ * dil), ATT_TQ, stride=dil)

    def norm_rope(x, w_ref, c, out_scale):
        rows = pl.ds(pl.multiple_of(c * ATT_TQ, ATT_TQ), ATT_TQ)
        w, w_rolled = w_ref[0:1, :], w_ref[1:2, :]
        xr = pltpu.roll(x, HEAD_DIM // 2, 1)
        y = x * (cos_ref[rows, :] * w) + xr * (sin_ref[rows, :] * w_rolled)
        sq = x * x
        hi = sq.astype(BF16)
        lo = (sq - hi.astype(F32)).astype(BF16)
        ssq = jnp.dot(jnp.concatenate([hi, lo], axis=1), sum_lanes, preferred_element_type=F32)
        return y * (lax.rsqrt(ssq * (1.0 / HEAD_DIM) + RMS_EPS) * out_scale)

    sum_lanes = jnp.ones((2 * HEAD_DIM, HEAD_DIM), BF16)
    zeros = jnp.zeros((ATT_RADIUS, 2 * HEAD_DIM), BF16)
    ones = jnp.ones((ATT_TQ, HEAD_DIM), BF16)

    def zero_pads(r, carry):
        lo = pl.multiple_of(r * seg, ATT_RADIUS)
        hi = pl.multiple_of(r * seg + ATT_RADIUS + n, ATT_RADIUS)
        kd_ref[pl.ds(lo, ATT_RADIUS), :] = zeros[:, :HEAD_DIM]
        kd_ref[pl.ds(hi, ATT_RADIUS), :] = zeros[:, :HEAD_DIM]
        vd_ref[pl.ds(lo, ATT_RADIUS), :] = zeros
        vd_ref[pl.ds(hi, ATT_RADIUS), :] = zeros
        return carry

    lax.fori_loop(0, dil, zero_pads, 0)

    def prep(c2, carry):
        for u in range(ATT_UNROLL):
            c = c2 * ATT_UNROLL + u
            r, mb, rows = rows_of(c)
            qn = norm_rope(q_ref[rows, :], qw_ref, c, scale)
            kn = norm_rope(k_ref[rows, :], kw_ref, c, 1.0)
            qd_ref[pl.ds(pl.multiple_of(c * ATT_TQ, ATT_TQ), ATT_TQ), :] = qn.astype(BF16)
            dst = pl.ds(pl.multiple_of(r * seg + ATT_RADIUS + mb * ATT_TQ, ATT_RADIUS), ATT_TQ)
            kd_ref[dst, :] = kn.astype(BF16)
            vd_ref[dst, :HEAD_DIM] = v_ref[rows, :].astype(BF16)
            vd_ref[dst, HEAD_DIM:] = ones
        return carry

    lax.fori_loop(0, s // (ATT_TQ * ATT_UNROLL), prep, 0)

    def block(c2, carry):
        parts = []
        for u in range(ATT_SCORE_UNROLL):
            c = c2 * ATT_SCORE_UNROLL + u
            r, mb, rows = rows_of(c)
            qn = qd_ref[pl.ds(pl.multiple_of(c * ATT_TQ, ATT_TQ), ATT_TQ), :]
            win = pl.ds(pl.multiple_of(r * seg + mb * ATT_TQ, ATT_RADIUS), ATT_TK)
            edge = jnp.where(mb == 0, 1, 0) + jnp.where(mb == nblk - 1, 2, 0)
            sc = lax.dot_general(qn, kd_ref[win, :], (((1,), (1,)), ((), ())), preferred_element_type=F32)
            sc = sc + bias_ref[edge]
            mx = jnp.max(sc, axis=-1, keepdims=True)
            p = jnp.exp(sc - mx).astype(BF16)
            pv = jnp.dot(p, vd_ref[win, :], preferred_element_type=F32)
            parts.append((rows, mx, pv))
        for rows, mx, pv in parts:
            m_b = jnp.broadcast_to(mx, (ATT_TQ, HEAD_DIM))
            pv, l_b = pv[:, :HEAD_DIM], pv[:, HEAD_DIM:]
            if first:
                acc_n, m_n, l_n = pv, m_b, l_b
            else:
                m_o = m_ref[rows, :]
                m_n = jnp.maximum(m_o, m_b)
                a, b = jnp.exp(m_o - m_n), jnp.exp(m_b - m_n)
                acc_n = a * acc_ref[rows, :] + b * pv
                l_n = a * l_ref[rows, :] + b * l_b
            acc_ref[rows, :] = acc_n
            m_ref[rows, :] = m_n
            l_ref[rows, :] = l_n
        return carry

    lax.fori_loop(0, s // (ATT_TQ * ATT_SCORE_UNROLL), block, 0)


def _attn_kernel(q_ref, k_ref, v_ref, cos_ref, sin_ref, qw_ref, kw_ref, *rest, s):
    o_ref, qd_ref, kd_ref, vd_ref, bias_ref, acc_ref, m_ref, l_ref = rest[-8:]
    g = pl.program_id(2)
    ng = len(ATTN_GROUPS)

    qi = lax.broadcasted_iota(jnp.int32, (ATT_TQ, ATT_TK), 0)
    kj = lax.broadcasted_iota(jnp.int32, (ATT_TQ, ATT_TK), 1)
    band = jnp.abs(kj - ATT_RADIUS - qi) <= ATT_RADIUS
    for e in range(4):
        ok = band
        if e & 1:
            ok = ok & (kj >= ATT_RADIUS)
        if e & 2:
            ok = ok & (kj < ATT_TQ + ATT_RADIUS)
        bias_ref[e] = jnp.where(ok, 0.0, NEG_INF)

    for step, (_, dil) in enumerate(reversed(ATTN_GROUPS)):
        @pl.when(g == step)
        def _(dil=dil, step=step):
            _attn_group(q_ref, k_ref, v_ref, cos_ref, sin_ref, qw_ref, kw_ref,
                        qd_ref, kd_ref, vd_ref, bias_ref, acc_ref, m_ref, l_ref,
                        s=s, dil=dil, first=step == 0)

    @pl.when(g == ng - 1)
    def _():
        rows_per = 2 * ATT_TQ

        def finish(c, carry):
            rows = pl.ds(pl.multiple_of(c * rows_per, rows_per), rows_per)
            o_ref[rows, :] = (acc_ref[rows, :] / l_ref[rows, :]).astype(o_ref.dtype)
            return carry

        lax.fori_loop(0, s // rows_per, finish, 0)


def attention(qkv, cos, sin, q_norm, k_norm, row0, b, s, prev=None):
    shared = _SharedOut(prev)
    blk0 = row0 // s
    hpg = HEADS_PER_GROUP
    max_dil = max(d for _, d in ATTN_GROUPS)
    kv_rows = s + 2 * ATT_RADIUS * max_dil

    def col(base):
        return lambda bi, j, g: (blk0 + bi, base + (len(ATTN_GROUPS) - 1 - g) * hpg + j)

    def with_rolled(w):
        return jnp.stack([w, jnp.roll(w, HEAD_DIM // 2)])

    return pl.pallas_call(
        functools.partial(_attn_kernel, s=s),
        grid=(b, hpg, len(ATTN_GROUPS)),
        in_specs=[pl.BlockSpec((s, HEAD_DIM), col(0)),
                  pl.BlockSpec((s, HEAD_DIM), col(ATT_HEADS)),
                  pl.BlockSpec((s, HEAD_DIM), col(2 * ATT_HEADS)),
                  pl.BlockSpec((None, s, HEAD_DIM), lambda bi, j, g: (len(ATTN_GROUPS) - 1 - g, 0, 0)),
                  pl.BlockSpec((None, s, HEAD_DIM), lambda bi, j, g: (len(ATTN_GROUPS) - 1 - g, 0, 0)),
                  _resident((2, HEAD_DIM)),
                  _resident((2, HEAD_DIM))] + shared.in_specs,
        out_specs=pl.BlockSpec((s, HEAD_DIM), lambda bi, j, g: (blk0 + bi, j)),
        out_shape=jax.ShapeDtypeStruct((qkv.shape[0], ATT_OUT), BF16),
        input_output_aliases=shared.aliases(7),
        scratch_shapes=[pltpu.VMEM((s, HEAD_DIM), BF16),
                        pltpu.VMEM((kv_rows, HEAD_DIM), BF16),
                        pltpu.VMEM((kv_rows, 2 * HEAD_DIM), BF16),
                        pltpu.VMEM((4, ATT_TQ, ATT_TK), F32),
                        pltpu.VMEM((s, HEAD_DIM), F32),
                        pltpu.VMEM((s, HEAD_DIM), F32),
                        pltpu.VMEM((s, HEAD_DIM), F32)],
        compiler_params=_params(("parallel", "parallel", "arbitrary")),
        name=f"dilated_attention_s{s}",
    )(qkv, qkv, qkv, cos, sin, with_rolled(q_norm), with_rolled(k_norm), *shared.args)


def rope_tables(s):
    inv_freq = ROPE_THETA ** (-jnp.arange(0, HEAD_DIM, 2, dtype=F32) / HEAD_DIM)
    ang = jnp.arange(s, dtype=F32)[:, None] * inv_freq[None, :]
    cos, sin = jnp.cos(ang), jnp.sin(ang)
    cos, sin = jnp.concatenate([cos, cos], axis=-1), jnp.concatenate([-sin, sin], axis=-1)

    def by_subsequence(t):
        return jnp.stack([t.reshape(s // d, d, HEAD_DIM).swapaxes(0, 1).reshape(s, HEAD_DIM)
                          for _, d in ATTN_GROUPS])

    return by_subsequence(cos), by_subsequence(sin)


def _scan_rows(x, reverse):
    row = lax.broadcasted_iota(jnp.int32, x.shape, 0)
    sh = 1
    while sh < CHUNK:
        if reverse:
            x = x + jnp.where(row < CHUNK - sh, pltpu.roll(x, CHUNK - sh, 0), 0.0)
        else:
            x = x + jnp.where(row >= sh, pltpu.roll(x, sh, 0), 0.0)
        sh *= 2
    return x


def _ssd_fwd_kernel(xbc_ref, xprev_ref, xnext_ref, dt_ref, cw_ref, cb_ref, dtb_ref, alog_ref,
                    y_ref, xs_ref, bc_ref, state_ref, *, nc, di, gn):
    chunk = pl.program_id(1)
    x = xbc_ref[...]
    row = lax.broadcasted_iota(jnp.int32, (SUBLANES, 1), 0)
    prev_row = jnp.where(chunk > 0, xprev_ref[SUBLANES - 1:SUBLANES, :], 0.0)
    next_row = jnp.where(chunk < nc - 1, xnext_ref[0:1, :], 0.0)
    xm1 = pltpu.roll(x, 1, 0)
    xm1 = jnp.concatenate([jnp.where(row == 0, prev_row, xm1[:SUBLANES]), xm1[SUBLANES:]], axis=0)
    xp1 = pltpu.roll(x, CHUNK - 1, 0)
    xp1 = jnp.concatenate([xp1[:-SUBLANES], jnp.where(row == SUBLANES - 1, next_row, xp1[-SUBLANES:])], axis=0)
    xc = xm1 * cw_ref[0:1, :] + x * cw_ref[1:2, :] + xp1 * cw_ref[2:3, :] + cb_ref[...]
    xc = xc * jax.nn.sigmoid(xc)
    xs_ref[...] = xc[:, :di]
    bc_ref[...] = xc[:, di:].astype(BF16)
    _ssd_chunk(xs_ref, bc_ref, dt_ref, dtb_ref, alog_ref, y_ref, state_ref, di=di, gn=gn, reverse=False)


def _ssd_bwd_kernel(xs_ref, bc_ref, dt_ref, dtb_ref, alog_ref, yf_ref, z_ref, dskip_ref, nw_ref,
                    *rest, di, gn):
    o_ref, state_ref, y_ref = rest[-3:]
    _ssd_chunk(xs_ref, bc_ref, dt_ref, dtb_ref, alog_ref, y_ref, state_ref, di=di, gn=gn, reverse=True)
    y = yf_ref[...] + y_ref[...] + dskip_ref[...] * xs_ref[...]
    z = z_ref[...]
    y = y * (z * jax.nn.sigmoid(z))
    o_ref[...] = ((y * _row_rms(y)) * nw_ref[...]).astype(o_ref.dtype)


def _ssd_chunk(xs_ref, bc_ref, dt_ref, dtb_ref, alog_ref, y_ref, state_ref, *, di, gn, reverse):
    gw = di // SSM_GROUPS
    n_state = gn // SSM_GROUPS
    half = SSM_HEADDIM

    @pl.when(pl.program_id(1) == 0)
    def _():
        state_ref[...] = jnp.zeros_like(state_ref)

    dtr = dt_ref[...] + dtb_ref[...]
    dt = jnp.maximum(dtr, 0.0) + jnp.log1p(jnp.exp(-jnp.abs(dtr)))
    acum = _scan_rows(dt * (-jnp.exp(alog_ref[...])), reverse)
    total = acum[0:1, :] if reverse else acum[CHUNK - 1:CHUNK, :]
    wend = jnp.exp(total - acum) * dt
    src_t = (acum - jnp.log(dt)).T

    li = lax.broadcasted_iota(jnp.int32, (CHUNK, CHUNK), 0)
    si = lax.broadcasted_iota(jnp.int32, (CHUNK, CHUNK), 1)
    causal = (si >= li) if reverse else (li >= si)
    lane = lax.broadcasted_iota(jnp.int32, (CHUNK, LANES), 1)
    lo_half = lane < half
    lo_mask = lo_half.astype(BF16)
    hi_mask = 1 - lo_mask

    def lanes_of(col_vals, h):
        return jnp.broadcast_to(col_vals[:, h:h + 1], (CHUNK, LANES))

    for g in range(SSM_GROUPS):
        b_g = bc_ref[:, g * n_state:(g + 1) * n_state]
        c_g = bc_ref[:, gn + g * n_state:gn + (g + 1) * n_state]
        cb = lax.dot_general(c_g, b_g, (((1,), (1,)), ((), ())), preferred_element_type=F32).astype(BF16)
        st = state_ref[g]
        y_off = jnp.dot(c_g, st.astype(BF16), preferred_element_type=F32)
        wx_parts, dec_parts = [], []
        for pr in range(gw // LANES):
            col0 = g * gw + pr * LANES
            h0 = col0 // half
            xp = xs_ref[:, col0:col0 + LANES]
            sc, ecol, wcol = [], [], []
            for h in (h0, h0 + 1):
                a_col = lanes_of(acum, h)
                dec = jnp.exp(jnp.where(causal, a_col - src_t[h:h + 1, :], -jnp.inf))
                sc.append(cb * dec.astype(BF16))
                ecol.append(jnp.exp(a_col))
                wcol.append(lanes_of(wend, h))
            s2 = jnp.concatenate(sc, axis=1)
            xp16 = xp.astype(BF16)
            x2 = jnp.concatenate([xp16 * lo_mask, xp16 * hi_mask], axis=0)
            y = jnp.dot(s2, x2, preferred_element_type=F32)
            y = y + y_off[:, pr * LANES:(pr + 1) * LANES] * jnp.where(lo_half, ecol[0], ecol[1])
            y_ref[:, col0:col0 + LANES] = y
            wx_parts.append((jnp.where(lo_half, wcol[0], wcol[1]) * xp).astype(BF16))
            tot0 = jnp.broadcast_to(total[:, h0:h0 + 1], (1, LANES))
            tot1 = jnp.broadcast_to(total[:, h0 + 1:h0 + 2], (1, LANES))
            dec_parts.append(jnp.exp(jnp.where(lo_half[0:1, :], tot0, tot1)))
        wx = jnp.concatenate(wx_parts, axis=1) if len(wx_parts) > 1 else wx_parts[0]
        sdec = jnp.concatenate(dec_parts, axis=1) if len(dec_parts) > 1 else dec_parts[0]
        upd = lax.dot_general(b_g, wx, (((0,), (0,)), ((), ())), preferred_element_type=F32)
        state_ref[g] = st * sdec + upd


def ssd_mixer(xbc, z_dt, conv_w, conv_b, dt_bias, a_log, d_skip_row, norm_w, row0, b, s, prev=None):
    shared = _SharedOut(prev)
    cd = conv_w.shape[1]
    heads = dt_bias.shape[-1]
    di = heads * SSM_HEADDIM
    gn = (cd - di) // 2
    nc = s // CHUNK
    cblk0 = row0 // CHUNK
    per8 = CHUNK // SUBLANES
    last8 = xbc.shape[0] // SUBLANES - 1
    pad = lambda v: jnp.zeros((1, LANES), F32).at[0, :heads].set(v)
    state = pltpu.VMEM((SSM_GROUPS, gn // SSM_GROUPS, di // SSM_GROUPS), F32)

    fwd_g = lambda bi, c: (cblk0 + bi * nc + c, 0)
    fwd_l = lambda bi, c: (bi * nc + c, 0)
    y_fwd, xs, bc = pl.pallas_call(
        functools.partial(_ssd_fwd_kernel, nc=nc, di=di, gn=gn),
        grid=(b, nc),
        in_specs=[pl.BlockSpec((CHUNK, cd), fwd_g),
                  pl.BlockSpec((SUBLANES, cd), lambda bi, c: (jnp.maximum(fwd_g(bi, c)[0] * per8 - 1, 0), 0)),
                  pl.BlockSpec((SUBLANES, cd), lambda bi, c: (jnp.minimum((fwd_g(bi, c)[0] + 1) * per8, last8), 0)),
                  pl.BlockSpec((CHUNK, LANES), lambda bi, c: (fwd_g(bi, c)[0], di // LANES)),
                  _resident((3, cd)), _resident((1, cd)), _resident((1, LANES)), _resident((1, LANES))],
        out_specs=[pl.BlockSpec((CHUNK, di), fwd_l), pl.BlockSpec((CHUNK, di), fwd_l),
                   pl.BlockSpec((CHUNK, 2 * gn), fwd_l)],
        out_shape=[jax.ShapeDtypeStruct((b * s, di), F32), jax.ShapeDtypeStruct((b * s, di), F32),
                   jax.ShapeDtypeStruct((b * s, 2 * gn), BF16)],
        scratch_shapes=[state],
        compiler_params=_params(("parallel", "arbitrary")),
        name=f"ssd_fwd_s{s}",
    )(xbc, xbc, xbc, z_dt, conv_w, conv_b.reshape(1, cd), pad(dt_bias[0]), pad(a_log[0]))

    bwd_g = lambda bi, c: (cblk0 + bi * nc + nc - 1 - c, 0)
    bwd_l = lambda bi, c: (bi * nc + nc - 1 - c, 0)
    return pl.pallas_call(
        functools.partial(_ssd_bwd_kernel, di=di, gn=gn),
        grid=(b, nc),
        in_specs=[pl.BlockSpec((CHUNK, di), bwd_l), pl.BlockSpec((CHUNK, 2 * gn), bwd_l),
                  pl.BlockSpec((CHUNK, LANES), lambda bi, c: (bwd_g(bi, c)[0], di // LANES + 1)),
                  _resident((1, LANES)), _resident((1, LANES)),
                  pl.BlockSpec((CHUNK, di), bwd_l), pl.BlockSpec((CHUNK, di), bwd_g),
                  _resident((1, di)), _resident((1, di))] + shared.in_specs,
        out_specs=pl.BlockSpec((CHUNK, di), bwd_g),
        out_shape=jax.ShapeDtypeStruct((xbc.shape[0], di), BF16),
        input_output_aliases=shared.aliases(9),
        scratch_shapes=[state, pltpu.VMEM((CHUNK, di), F32)],
        compiler_params=_params(("parallel", "arbitrary")),
        name=f"ssd_bwd_s{s}",
    )(xs, bc, z_dt, pad(dt_bias[1]), pad(a_log[1]), y_fwd, z_dt, d_skip_row, norm_w.reshape(1, di), *shared.args)


def _derived_weights(p):
    heads = p["dt_bias"].shape[-1]
    di = p["ssm_norm"].shape[-1]
    cd = p["conv_w"].shape[-1]
    w_in = p["w_in"]
    depth, d, _ = w_in.shape
    o_z = 3 * ATT_W
    o_xbc = o_z + di
    o_dt = o_xbc + cd
    o_gate = o_dt + 2 * heads
    bf = lambda a: a.astype(BF16)

    def z_dt_cols(wl):
        pad = jnp.zeros((d, LANES - heads), F32)
        return jnp.concatenate([wl[:, o_z:o_xbc], wl[:, o_dt:o_dt + heads], pad,
                                wl[:, o_dt + heads:o_gate], pad], axis=-1)

    split = lambda cols: [bf(cols(w_in[l]))[None] for l in range(depth)]
    return dict(
        ffn1_w_gate=bf(p["ffn1_w_gate"]), ffn1_w_up=bf(p["ffn1_w_up"]), ffn1_w_down=bf(p["ffn1_w_down"]),
        ffn2_w_gate=bf(p["ffn2_w_gate"]), ffn2_w_up=bf(p["ffn2_w_up"]), ffn2_w_down=bf(p["ffn2_w_down"]),
        w_qkv=split(lambda wl: wl[:, :o_z]), w_xbc=split(lambda wl: wl[:, o_xbc:o_dt]),
        w_z_dt=split(z_dt_cols), w_gate=split(lambda wl: wl[:, o_gate:]),
        w_attn_out=bf(p["w_attn_out"]), w_ssm_out=bf(p["w_ssm_out"]), w_out=bf(p["w_out"]),
        d_skip_row=jnp.repeat(p["d_skip"], SSM_HEADDIM, axis=-1),
    )


def _layer(x, xw, r, p, c, layer, sets, ropes):
    a = ffn_up(xw, r, c["ffn1_w_gate"], c["ffn1_w_up"], layer)
    x, xw, r = ffn_down(a, c["ffn1_w_down"], layer, x, 0.5, p["mix_norm"][layer])
    qkv = matmul_scaled(xw, r, c["w_qkv"][layer], 0, tn=768, name="proj_qkv")
    z_dt = matmul_scaled(xw, r, c["w_z_dt"][layer], 0, tn=768, name="proj_z_dt")
    xbc = matmul_scaled(xw, r, c["w_xbc"][layer], 0, name="proj_xbc")
    gate_raw = matmul_scaled(xw, r, c["w_gate"][layer], 0, name="proj_gate")
    y_att = y_ssm = None
    for (row0, b, s), (cos, sin) in zip(sets, ropes):
        y_att = attention(qkv, cos, sin, p["q_norm"][layer], p["k_norm"][layer], row0, b, s, prev=y_att)
        y_ssm = ssd_mixer(xbc, z_dt, p["conv_w"][layer], p["conv_b"][layer], p["dt_bias"][layer],
                          p["a_log"][layer], c["d_skip_row"][layer:layer + 1], p["ssm_norm"][layer],
                          row0, b, s, prev=y_ssm)
    mix = gated_mix(y_att, y_ssm, c["w_attn_out"], c["w_ssm_out"], layer, gate_raw)
    x, xw, r = mix_out(mix, c["w_out"], layer, x, p["ffn2_norm"][layer])
    return x, ffn_up(xw, r, c["ffn2_w_gate"], c["ffn2_w_up"], layer)


def kernel(x_prompt, x_sample, ffn1_norm, ffn1_w_gate, ffn1_w_up, ffn1_w_down, mix_norm, w_in, q_norm, k_norm, conv_w, conv_b, dt_bias, a_log, d_skip, ssm_norm, w_attn_out, w_ssm_out, w_out, ffn2_norm, ffn2_w_gate, ffn2_w_up, ffn2_w_down):
    params = dict(ffn1_norm=ffn1_norm, ffn1_w_gate=ffn1_w_gate, ffn1_w_up=ffn1_w_up, ffn1_w_down=ffn1_w_down,
                  mix_norm=mix_norm, w_in=w_in, q_norm=q_norm, k_norm=k_norm, conv_w=conv_w, conv_b=conv_b,
                  dt_bias=dt_bias, a_log=a_log, d_skip=d_skip, ssm_norm=ssm_norm, w_attn_out=w_attn_out,
                  w_ssm_out=w_ssm_out, w_out=w_out, ffn2_norm=ffn2_norm, ffn2_w_gate=ffn2_w_gate,
                  ffn2_w_up=ffn2_w_up, ffn2_w_down=ffn2_w_down)
    d = x_prompt.shape[-1]
    bp, sp = x_prompt.shape[:2]
    bs, ss = x_sample.shape[:2]
    mp = bp * sp
    depth = ffn1_norm.shape[0]
    sets = ((0, bp, sp), (mp, bs, ss))
    ropes = (rope_tables(sp), rope_tables(ss))
    c = _derived_weights(params)
    x = jnp.concatenate([x_prompt.reshape(mp, d), x_sample.reshape(bs * ss, d)])
    xw, r = norm_prep(x, ffn1_norm[0])
    for layer in range(depth - 1):
        x, a = _layer(x, xw, r, params, c, layer, sets, ropes)
        x, xw, r = ffn_down(a, c["ffn2_w_down"], layer, x, 0.5, ffn1_norm[layer + 1])
    x, a = _layer(x, xw, r, params, c, depth - 1, sets, ropes)
    y_prompt = ffn_down(a, c["ffn2_w_down"], depth - 1, x, 0.5, row0=0, rows=mp)
    y_sample = ffn_down(a, c["ffn2_w_down"], depth - 1, x, 0.5, row0=mp, rows=bs * ss)
    return y_prompt.reshape(x_prompt.shape), y_sample.reshape(x_sample.shape)
```

```python
import functools

import jax
import jax.numpy as jnp
from jax import lax
from jax.experimental import pallas as pl
from jax.experimental.pallas import tpu as pltpu

HEAD_DIM = 128
ATTN_GROUPS = ((128, 1), (512, 4), (2048, 16))
HEADS_PER_GROUP = 4
ATT_HEADS = HEADS_PER_GROUP * len(ATTN_GROUPS)
ATT_W = ATT_HEADS * HEAD_DIM
ATT_OUT = HEADS_PER_GROUP * HEAD_DIM
ATT_RADIUS = 64
ROPE_THETA = 10000.0
SSM_GROUPS = 4
SSM_HEADDIM = 64
CHUNK = 128
RMS_EPS = 1e-6
NEG_INF = -1e30

LANES = 128
SUBLANES = 8
V7X_VMEM_LIMIT = 56 * 1024 * 1024

F32 = jnp.float32
BF16 = jnp.bfloat16

assert all(w // (2 * d) == ATT_RADIUS for w, d in ATTN_GROUPS)


def _params(semantics):
    return pltpu.CompilerParams(dimension_semantics=semantics, vmem_limit_bytes=V7X_VMEM_LIMIT)


def _tile(n, pref):
    t = min(pref, n)
    t -= t % LANES
    while n % t:
        t -= LANES
    return t


def _resident(shape):
    return pl.BlockSpec(shape, lambda *_: (0,) * len(shape), pipeline_mode=pl.Buffered(1))


class _SharedOut:
    def __init__(self, prev):
        self.args = [] if prev is None else list(prev) if isinstance(prev, (tuple, list)) else [prev]
        self.in_specs = [pl.BlockSpec(memory_space=pl.ANY)] * len(self.args)

    def aliases(self, operand_index):
        return {operand_index + k: k for k in range(len(self.args))}


def _resident_layer(w, layer):
    return pl.BlockSpec((None,) + w.shape[1:], lambda *_: (layer, 0, 0), pipeline_mode=pl.Buffered(1))


def _row_rms(x):
    return lax.rsqrt(jnp.mean(x * x, axis=-1, keepdims=True) + RMS_EPS)


def _norm_prep_kernel(x_ref, w_ref, *rest):
    xw_ref, r_ref = rest[-2:]
    x = x_ref[...]
    xw_ref[...] = (x * w_ref[...]).astype(BF16)
    r_ref[...] = jnp.broadcast_to(_row_rms(x), r_ref.shape)


def norm_prep(x, w, out_row0, out_rows, prev=None, tm=256):
    m, d = x.shape
    shared = _SharedOut(prev)
    dst = lambda i: (i + out_row0 // tm, 0)
    return pl.pallas_call(
        _norm_prep_kernel,
        grid=(m // tm,),
        in_specs=[pl.BlockSpec((tm, d), lambda i: (i, 0)), _resident((1, d))] + shared.in_specs,
        out_specs=[pl.BlockSpec((tm, d), dst), pl.BlockSpec((tm, LANES), dst)],
        out_shape=[jax.ShapeDtypeStruct((out_rows, d), BF16), jax.ShapeDtypeStruct((out_rows, LANES), F32)],
        input_output_aliases=shared.aliases(2),
        compiler_params=_params(("parallel",)),
        name="norm_prep",
    )(x, w.reshape(1, d), *shared.args)


def _layer_cols(w, layer, tn):
    return pl.BlockSpec((None, w.shape[1], tn), lambda i, j: (layer, 0, j))


def _mm_kernel(a_ref, r_ref, w_ref, o_ref):
    y = jnp.dot(a_ref[...], w_ref[...], preferred_element_type=F32)
    o_ref[...] = (y * r_ref[:, 0:1]).astype(o_ref.dtype)


def matmul_scaled(a, r, w, layer, tm=1024, tn=1024, name="matmul"):
    m, k = a.shape
    n = w.shape[2]
    tm, tn = min(tm, m), _tile(n, tn)
    return pl.pallas_call(
        _mm_kernel,
        grid=(m // tm, n // tn),
        in_specs=[pl.BlockSpec((tm, k), lambda i, j: (i, 0)),
                  pl.BlockSpec((tm, LANES), lambda i, j: (i, 0)),
                  _layer_cols(w, layer, tn)],
        out_specs=pl.BlockSpec((tm, tn), lambda i, j: (i, j)),
        out_shape=jax.ShapeDtypeStruct((m, n), F32),
        compiler_params=_params(("parallel", "arbitrary")),
        name=name,
    )(a, r, w)


def _ffn_up_kernel(h_ref, r_ref, wg_ref, wu_ref, o_ref):
    h = h_ref[...]
    r = r_ref[:, 0:1]
    g = jnp.dot(h, wg_ref[...], preferred_element_type=F32) * r
    u = jnp.dot(h, wu_ref[...], preferred_element_type=F32) * r
    o_ref[...] = (g * jax.nn.sigmoid(g) * u).astype(o_ref.dtype)


def ffn_up(h, r, wg, wu, layer, tm=1024, tn=512):
    m, k = h.shape
    n = wg.shape[2]
    tm, tn = min(tm, m), _tile(n, tn)
    return pl.pallas_call(
        _ffn_up_kernel,
        grid=(m // tm, n // tn),
        in_specs=[pl.BlockSpec((tm, k), lambda i, j: (i, 0)),
                  pl.BlockSpec((tm, LANES), lambda i, j: (i, 0)),
                  _layer_cols(wg, layer, tn),
                  _layer_cols(wu, layer, tn)],
        out_specs=pl.BlockSpec((tm, tn), lambda i, j: (i, j)),
        out_shape=jax.ShapeDtypeStruct((m, n), BF16),
        compiler_params=_params(("parallel", "arbitrary")),
        name="ffn_up",
    )(h, r, wg, wu)


def _ffn_down_kernel(a_ref, w_ref, x_ref, *rest, scale, emit_norm):
    xn = x_ref[...] + scale * jnp.dot(a_ref[...], w_ref[...], preferred_element_type=F32)
    if emit_norm:
        wn_ref = rest[0]
        o_ref, xw_ref, r_ref = rest[-3:]
        xw_ref[...] = (xn * wn_ref[...]).astype(BF16)
        r_ref[...] = jnp.broadcast_to(_row_rms(xn), r_ref.shape)
    else:
        o_ref = rest[-1]
    o_ref[...] = xn


def ffn_down(a, w, layer, x, scale, next_norm_w=None, rows=None, a_row0=0, x_row0=0, out_row0=0, out_rows=None,
             prev=None, tm=256):
    k = a.shape[1]
    m = a.shape[0] if rows is None else rows
    out_rows = m if out_rows is None else out_rows
    n = w.shape[2]
    emit = next_norm_w is not None
    shared = _SharedOut(prev)
    at = lambda row0: (lambda i: (i + row0 // tm, 0))
    in_specs = [pl.BlockSpec((tm, k), at(a_row0)), _resident_layer(w, layer), pl.BlockSpec((tm, n), at(x_row0))]
    out_specs = [pl.BlockSpec((tm, n), at(out_row0))]
    out_shape = [jax.ShapeDtypeStruct((out_rows, n), F32)]
    args = [a, w, x]
    if emit:
        in_specs.append(_resident((1, n)))
        args.append(next_norm_w.reshape(1, n))
        out_specs += [pl.BlockSpec((tm, n), at(out_row0)), pl.BlockSpec((tm, LANES), at(out_row0))]
        out_shape += [jax.ShapeDtypeStruct((out_rows, n), BF16), jax.ShapeDtypeStruct((out_rows, LANES), F32)]
    out = pl.pallas_call(
        functools.partial(_ffn_down_kernel, scale=scale, emit_norm=emit),
        grid=(m // tm,),
        in_specs=in_specs + shared.in_specs,
        out_specs=out_specs,
        out_shape=out_shape,
        input_output_aliases=shared.aliases(len(args)),
        compiler_params=_params(("parallel",)),
        name="ffn_down",
    )(*args, *shared.args)
    return out if emit else out[0]


def _mix_out_kernel(a_ref, w_ref, x_ref, wn_ref, o_ref, xw_ref, r_ref, ssq_ref, *, d):
    j = pl.program_id(1)
    xn = x_ref[...] + jnp.dot(a_ref[...], w_ref[...], preferred_element_type=F32)
    o_ref[...] = xn
    xw_ref[...] = (xn * wn_ref[...]).astype(BF16)
    part = jnp.broadcast_to(jnp.sum(xn * xn, axis=-1, keepdims=True), ssq_ref.shape)

    @pl.when(j == 0)
    def _():
        ssq_ref[...] = part

    @pl.when(j > 0)
    def _():
        ssq_ref[...] += part

    @pl.when(j == pl.num_programs(1) - 1)
    def _():
        r_ref[...] = lax.rsqrt(ssq_ref[...] / d + RMS_EPS)


def mix_out(a, w, layer, x, next_norm_w, tm=1024, tn=512):
    m, k = a.shape
    n = w.shape[2]
    tm, tn = min(tm, m), _tile(n, tn)
    return pl.pallas_call(
        functools.partial(_mix_out_kernel, d=n),
        grid=(m // tm, n // tn),
        in_specs=[pl.BlockSpec((tm, k), lambda i, j: (i, 0)),
                  _layer_cols(w, layer, tn),
                  pl.BlockSpec((tm, tn), lambda i, j: (i, j)),
                  pl.BlockSpec((1, tn), lambda i, j: (0, j))],
        out_specs=[pl.BlockSpec((tm, tn), lambda i, j: (i, j)),
                   pl.BlockSpec((tm, tn), lambda i, j: (i, j)),
                   pl.BlockSpec((tm, LANES), lambda i, j: (i, 0))],
        out_shape=[jax.ShapeDtypeStruct((m, n), F32), jax.ShapeDtypeStruct((m, n), BF16),
                   jax.ShapeDtypeStruct((m, LANES), F32)],
        scratch_shapes=[pltpu.VMEM((tm, LANES), F32)],
        compiler_params=_params(("parallel", "arbitrary")),
        name="mix_out",
    )(a, w, x, next_norm_w.reshape(1, n))


def _gated_mix_kernel(ya_ref, ys_ref, wa_ref, ws_ref, ga_ref, gs_ref, o_ref):
    pa = jnp.dot(ya_ref[...], wa_ref[...], preferred_element_type=F32)
    ps = jnp.dot(ys_ref[...], ws_ref[...], preferred_element_type=F32)
    o_ref[...] = (jax.nn.sigmoid(ga_ref[...]) * pa + jax.nn.sigmoid(gs_ref[...]) * ps).astype(o_ref.dtype)


def gated_mix(ya, ys, wa, ws, layer, gate_raw, tm=256):
    m, ka = ya.shape
    ks = ys.shape[1]
    n = wa.shape[2]
    row = lambda i: (i, 0)
    return pl.pallas_call(
        _gated_mix_kernel,
        grid=(m // tm,),
        in_specs=[pl.BlockSpec((tm, ka), row), pl.BlockSpec((tm, ks), row),
                  _resident_layer(wa, layer), _resident_layer(ws, layer),
                  pl.BlockSpec((tm, n), row), pl.BlockSpec((tm, n), lambda i: (i, 1))],
        out_specs=pl.BlockSpec((tm, n), row),
        out_shape=jax.ShapeDtypeStruct((m, n), BF16),
        compiler_params=_params(("parallel",)),
        name="gated_mix",
    )(ya, ys, wa, ws, gate_raw, gate_raw)


ATT_TQ = 128
ATT_TK = ATT_TQ + 2 * ATT_RADIUS
ATT_UNROLL = 8
ATT_SCORE_UNROLL = 8


def _attn_group(q_ref, k_ref, v_ref, cos_ref, sin_ref, qw_ref, kw_ref,
                qd_ref, kd_ref, vd_ref, bias_ref, acc_ref, m_ref, l_ref, *, s, dil, first):
    n = s // dil
    nblk = n // ATT_TQ
    seg = n + 2 * ATT_RADIUS
    scale = HEAD_DIM ** -0.5

    def rows_of(c):
        r, mb = c // nblk, c % nblk
        if dil == 1:
            return r, mb, pl.ds(pl.multiple_of(c * ATT_TQ, ATT_TQ), ATT_TQ)
        return r, mb, pl.ds(r + mb * (ATT_TQ * dil), ATT_TQ, stride=dil)

    def norm_rope(x, w_ref, c, out_scale):
        rows = pl.ds(pl.multiple_of(c * ATT_TQ, ATT_TQ), ATT_TQ)
        w, w_rolled = w_ref[0:1, :], w_ref[1:2, :]
        xr = pltpu.roll(x, HEAD_DIM // 2, 1)
        y = x * (cos_ref[rows, :] * w) + xr * (sin_ref[rows, :] * w_rolled)
        sq = x * x
        hi = sq.astype(BF16)
        lo = (sq - hi.astype(F32)).astype(BF16)
        ssq = jnp.dot(jnp.concatenate([hi, lo], axis=1), sum_lanes, preferred_element_type=F32)
        return y * (lax.rsqrt(ssq * (1.0 / HEAD_DIM) + RMS_EPS) * out_scale)

    sum_lanes = jnp.ones((2 * HEAD_DIM, HEAD_DIM), BF16)
    zeros = jnp.zeros((ATT_RADIUS, 2 * HEAD_DIM), BF16)
    ones = jnp.ones((ATT_TQ, HEAD_DIM), BF16)

    def zero_pads(r, carry):
        lo = pl.multiple_of(r * seg, ATT_RADIUS)
        hi = pl.multiple_of(r * seg + ATT_RADIUS + n, ATT_RADIUS)
        kd_ref[pl.ds(lo, ATT_RADIUS), :] = zeros[:, :HEAD_DIM]
        kd_ref[pl.ds(hi, ATT_RADIUS), :] = zeros[:, :HEAD_DIM]
        vd_ref[pl.ds(lo, ATT_RADIUS), :] = zeros
        vd_ref[pl.ds(hi, ATT_RADIUS), :] = zeros
        return carry

    lax.fori_loop(0, dil, zero_pads, 0)

    def prep(c2, carry):
        for u in range(ATT_UNROLL):
            c = c2 * ATT_UNROLL + u
            r, mb, rows = rows_of(c)
            qn = norm_rope(q_ref[rows, :], qw_ref, c, scale)
            kn = norm_rope(k_ref[rows, :], kw_ref, c, 1.0)
            qd_ref[pl.ds(pl.multiple_of(c * ATT_TQ, ATT_TQ), ATT_TQ), :] = qn.astype(BF16)
            dst = pl.ds(pl.multiple_of(r * seg + ATT_RADIUS + mb * ATT_TQ, ATT_RADIUS), ATT_TQ)
            kd_ref[dst, :] = kn.astype(BF16)
            vd_ref[dst, :HEAD_DIM] = v_ref[rows, :].astype(BF16)
            vd_ref[dst, HEAD_DIM:] = ones
        return carry

    lax.fori_loop(0, s // (ATT_TQ * ATT_UNROLL), prep, 0)

    def block(c2, carry):
        parts = []
        for u in range(ATT_SCORE_UNROLL):
            c = c2 * ATT_SCORE_UNROLL + u
            r, mb, rows = rows_of(c)
            qn = qd_ref[pl.ds(pl.multiple_of(c * ATT_TQ, ATT_TQ), ATT_TQ), :]
            win = pl.ds(pl.multiple_of(r * seg + mb * ATT_TQ, ATT_RADIUS), ATT_TK)
            edge = jnp.where(mb == 0, 1, 0) + jnp.where(mb == nblk - 1, 2, 0)
            sc = lax.dot_general(qn, kd_ref[win, :], (((1,), (1,)), ((), ())), preferred_element_type=F32)
            sc = sc + bias_ref[edge]
            mx = jnp.max(sc, axis=-1, keepdims=True)
            p = jnp.exp(sc - mx).astype(BF16)
            pv = jnp.dot(p, vd_ref[win, :], preferred_element_type=F32)
            parts.append((rows, mx, pv))
        for rows, mx, pv in parts:
            m_b = jnp.broadcast_to(mx, (ATT_TQ, HEAD_DIM))
            pv, l_b = pv[:, :HEAD_DIM], pv[:, HEAD_DIM:]
            if first:
                acc_n, m_n, l_n = pv, m_b, l_b
            else:
                m_o = m_ref[rows, :]
                m_n = jnp.maximum(m_o, m_b)
                a, b = jnp.exp(m_o - m_n), jnp.exp(m_b - m_n)
                acc_n = a * acc_ref[rows, :] + b * pv
                l_n = a * l_ref[rows, :] + b * l_b
            acc_ref[rows, :] = acc_n
            m_ref[rows, :] = m_n
            l_ref[rows, :] = l_n
        return carry

    lax.fori_loop(0, s // (ATT_TQ * ATT_SCORE_UNROLL), block, 0)


def _attn_kernel(q_ref, k_ref, v_ref, cos_ref, sin_ref, qw_ref, kw_ref, *rest, s):
    o_ref, qd_ref, kd_ref, vd_ref, bias_ref, acc_ref, m_ref, l_ref = rest[-8:]
    g = pl.program_id(2)
    ng = len(ATTN_GROUPS)

    qi = lax.broadcasted_iota(jnp.int32, (ATT_TQ, ATT_TK), 0)
    kj = lax.broadcasted_iota(jnp.int32, (ATT_TQ, ATT_TK), 1)
    band = jnp.abs(kj - ATT_RADIUS - qi) <= ATT_RADIUS
    for e in range(4):
        ok = band
        if e & 1:
            ok = ok & (kj >= ATT_RADIUS)
        if e & 2:
            ok = ok & (kj < ATT_TQ + ATT_RADIUS)
        bias_ref[e] = jnp.where(ok, 0.0, NEG_INF)

    for step, (_, dil) in enumerate(reversed(ATTN_GROUPS)):
        @pl.when(g == step)
        def _(dil=dil, step=step):
            _attn_group(q_ref, k_ref, v_ref, cos_ref, sin_ref, qw_ref, kw_ref,
                        qd_ref, kd_ref, vd_ref, bias_ref, acc_ref, m_ref, l_ref,
                        s=s, dil=dil, first=step == 0)

    @pl.when(g == ng - 1)
    def _():
        rows_per = 2 * ATT_TQ

        def finish(c, carry):
            rows = pl.ds(pl.multiple_of(c * rows_per, rows_per), rows_per)
            o_ref[rows, :] = (acc_ref[rows, :] / l_ref[rows, :]).astype(o_ref.dtype)
            return carry

        lax.fori_loop(0, s // rows_per, finish, 0)


def attention(qkv, cos, sin, q_norm, k_norm, row0, b, s, prev=None):
    shared = _SharedOut(prev)
    blk0 = row0 // s
    hpg = HEADS_PER_GROUP
    max_dil = max(d for _, d in ATTN_GROUPS)
    kv_rows = s + 2 * ATT_RADIUS * max_dil

    def col(base):
        return lambda bi, j, g: (blk0 + bi, base + (len(ATTN_GROUPS) - 1 - g) * hpg + j)

    def with_rolled(w):
        return jnp.stack([w, jnp.roll(w, HEAD_DIM // 2)])

    return pl.pallas_call(
        functools.partial(_attn_kernel, s=s),
        grid=(b, hpg, len(ATTN_GROUPS)),
        in_specs=[pl.BlockSpec((s, HEAD_DIM), col(0)),
                  pl.BlockSpec((s, HEAD_DIM), col(ATT_HEADS)),
                  pl.BlockSpec((s, HEAD_DIM), col(2 * ATT_HEADS)),
                  pl.BlockSpec((None, s, HEAD_DIM), lambda bi, j, g: (len(ATTN_GROUPS) - 1 - g, 0, 0)),
                  pl.BlockSpec((None, s, HEAD_DIM), lambda bi, j, g: (len(ATTN_GROUPS) - 1 - g, 0, 0)),
                  _resident((2, HEAD_DIM)),
                  _resident((2, HEAD_DIM))] + shared.in_specs,
        out_specs=pl.BlockSpec((s, HEAD_DIM), lambda bi, j, g: (blk0 + bi, j)),
        out_shape=jax.ShapeDtypeStruct((qkv.shape[0], ATT_OUT), BF16),
        input_output_aliases=shared.aliases(7),
        scratch_shapes=[pltpu.VMEM((s, HEAD_DIM), BF16),
                        pltpu.VMEM((kv_rows, HEAD_DIM), BF16),
                        pltpu.VMEM((kv_rows, 2 * HEAD_DIM), BF16),
                        pltpu.VMEM((4, ATT_TQ, ATT_TK), F32),
                        pltpu.VMEM((s, HEAD_DIM), F32),
                        pltpu.VMEM((s, HEAD_DIM), F32),
                        pltpu.VMEM((s, HEAD_DIM), F32)],
        compiler_params=_params(("parallel", "parallel", "arbitrary")),
        name=f"dilated_attention_s{s}",
    )(qkv, qkv, qkv, cos, sin, with_rolled(q_norm), with_rolled(k_norm), *shared.args)


def rope_tables(s):
    inv_freq = ROPE_THETA ** (-jnp.arange(0, HEAD_DIM, 2, dtype=F32) / HEAD_DIM)
    ang = jnp.arange(s, dtype=F32)[:, None] * inv_freq[None, :]
    cos, sin = jnp.cos(ang), jnp.sin(ang)
    cos, sin = jnp.concatenate([cos, cos], axis=-1), jnp.concatenate([-sin, sin], axis=-1)

    def by_subsequence(t):
        return jnp.stack([t.reshape(s // d, d, HEAD_DIM).swapaxes(0, 1).reshape(s, HEAD_DIM)
                          for _, d in ATTN_GROUPS])

    return by_subsequence(cos), by_subsequence(sin)


def _scan_rows(x, reverse):
    row = lax.broadcasted_iota(jnp.int32, x.shape, 0)
    sh = 1
    while sh < CHUNK:
        if reverse:
            x = x + jnp.where(row < CHUNK - sh, pltpu.roll(x, CHUNK - sh, 0), 0.0)
        else:
            x = x + jnp.where(row >= sh, pltpu.roll(x, sh, 0), 0.0)
        sh *= 2
    return x


def _ssd_fwd_kernel(xbc_ref, xprev_ref, xnext_ref, dt_ref, cw_ref, cb_ref, dtb_ref, alog_ref,
                    y_ref, xs_ref, bc_ref, state_ref, *, nc, di, gn):
    chunk = pl.program_id(1)
    x = xbc_ref[...]
    row = lax.broadcasted_iota(jnp.int32, (SUBLANES, 1), 0)
    prev_row = jnp.where(chunk > 0, xprev_ref[SUBLANES - 1:SUBLANES, :], 0.0)
    next_row = jnp.where(chunk < nc - 1, xnext_ref[0:1, :], 0.0)
    xm1 = pltpu.roll(x, 1, 0)
    xm1 = jnp.concatenate([jnp.where(row == 0, prev_row, xm1[:SUBLANES]), xm1[SUBLANES:]], axis=0)
    xp1 = pltpu.roll(x, CHUNK - 1, 0)
    xp1 = jnp.concatenate([xp1[:-SUBLANES], jnp.where(row == SUBLANES - 1, next_row, xp1[-SUBLANES:])], axis=0)
    xc = xm1 * cw_ref[0:1, :] + x * cw_ref[1:2, :] + xp1 * cw_ref[2:3, :] + cb_ref[...]
    xc = xc * jax.nn.sigmoid(xc)
    xs_ref[...] = xc[:, :di]
    bc_ref[...] = xc[:, di:].astype(BF16)
    _ssd_chunk(xs_ref, bc_ref, dt_ref, dtb_ref, alog_ref, y_ref, state_ref, di=di, gn=gn, reverse=False)


def _ssd_bwd_kernel(xs_ref, bc_ref, dt_ref, dtb_ref, alog_ref, yf_ref, z_ref, dskip_ref, nw_ref,
                    *rest, di, gn):
    o_ref, state_ref, y_ref = rest[-3:]
    _ssd_chunk(xs_ref, bc_ref, dt_ref, dtb_ref, alog_ref, y_ref, state_ref, di=di, gn=gn, reverse=True)
    y = yf_ref[...] + y_ref[...] + dskip_ref[...] * xs_ref[...]
    z = z_ref[...]
    y = y * (z * jax.nn.sigmoid(z))
    o_ref[...] = ((y * _row_rms(y)) * nw_ref[...]).astype(o_ref.dtype)


def _ssd_chunk(xs_ref, bc_ref, dt_ref, dtb_ref, alog_ref, y_ref, state_ref, *, di, gn, reverse):
    gw = di // SSM_GROUPS
    n_state = gn // SSM_GROUPS
    half = SSM_HEADDIM

    @pl.when(pl.program_id(1) == 0)
    def _():
        state_ref[...] = jnp.zeros_like(state_ref)

    dtr = dt_ref[...] + dtb_ref[...]
    dt = jnp.maximum(dtr, 0.0) + jnp.log1p(jnp.exp(-jnp.abs(dtr)))
    acum = _scan_rows(dt * (-jnp.exp(alog_ref[...])), reverse)
    total = acum[0:1, :] if reverse else acum[CHUNK - 1:CHUNK, :]
    wend = jnp.exp(total - acum) * dt
    src_t = (acum - jnp.log(dt)).T

    li = lax.broadcasted_iota(jnp.int32, (CHUNK, CHUNK), 0)
    si = lax.broadcasted_iota(jnp.int32, (CHUNK, CHUNK), 1)
    causal = (si >= li) if reverse else (li >= si)
    lane = lax.broadcasted_iota(jnp.int32, (CHUNK, LANES), 1)
    lo_half = lane < half
    lo_mask = lo_half.astype(BF16)
    hi_mask = 1 - lo_mask

    def lanes_of(col_vals, h):
        return jnp.broadcast_to(col_vals[:, h:h + 1], (CHUNK, LANES))

    for g in range(SSM_GROUPS):
        b_g = bc_ref[:, g * n_state:(g + 1) * n_state]
        c_g = bc_ref[:, gn + g * n_state:gn + (g + 1) * n_state]
        cb = lax.dot_general(c_g, b_g, (((1,), (1,)), ((), ())), preferred_element_type=F32).astype(BF16)
        st = state_ref[g]
        y_off = jnp.dot(c_g, st.astype(BF16), preferred_element_type=F32)
        wx_parts, dec_parts = [], []
        for pr in range(gw // LANES):
            col0 = g * gw + pr * LANES
            h0 = col0 // half
            xp = xs_ref[:, col0:col0 + LANES]
            sc, ecol, wcol = [], [], []
            for h in (h0, h0 + 1):
                a_col = lanes_of(acum, h)
                dec = jnp.exp(jnp.where(causal, a_col - src_t[h:h + 1, :], -jnp.inf))
                sc.append(cb * dec.astype(BF16))
                ecol.append(jnp.exp(a_col))
                wcol.append(lanes_of(wend, h))
            s2 = jnp.concatenate(sc, axis=1)
            xp16 = xp.astype(BF16)
            x2 = jnp.concatenate([xp16 * lo_mask, xp16 * hi_mask], axis=0)
            y = jnp.dot(s2, x2, preferred_element_type=F32)
            y = y + y_off[:, pr * LANES:(pr + 1) * LANES] * jnp.where(lo_half, ecol[0], ecol[1])
            y_ref[:, col0:col0 + LANES] = y
            wx_parts.append((jnp.where(lo_half, wcol[0], wcol[1]) * xp).astype(BF16))
            tot0 = jnp.broadcast_to(total[:, h0:h0 + 1], (1, LANES))
            tot1 = jnp.broadcast_to(total[:, h0 + 1:h0 + 2], (1, LANES))
            dec_parts.append(jnp.exp(jnp.where(lo_half[0:1, :], tot0, tot1)))
        wx = jnp.concatenate(wx_parts, axis=1) if len(wx_parts) > 1 else wx_parts[0]
        sdec = jnp.concatenate(dec_parts, axis=1) if len(dec_parts) > 1 else dec_parts[0]
        upd = lax.dot_general(b_g, wx, (((0,), (0,)), ((), ())), preferred_element_type=F32)
        state_ref[g] = st * sdec + upd


def ssd_mixer(xbc, z_dt, conv_w, conv_b, dt_bias, a_log, d_skip_row, norm_w, row0, b, s, prev=None):
    shared = _SharedOut(prev)
    cd = conv_w.shape[1]
    heads = dt_bias.shape[-1]
    di = heads * SSM_HEADDIM
    gn = (cd - di) // 2
    nc = s // CHUNK
    cblk0 = row0 // CHUNK
    per8 = CHUNK // SUBLANES
    last8 = xbc.shape[0] // SUBLANES - 1
    pad = lambda v: jnp.zeros((1, LANES), F32).at[0, :heads].set(v)
    state = pltpu.VMEM((SSM_GROUPS, gn // SSM_GROUPS, di // SSM_GROUPS), F32)

    fwd_g = lambda bi, c: (cblk0 + bi * nc + c, 0)
    fwd_l = lambda bi, c: (bi * nc + c, 0)
    y_fwd, xs, bc = pl.pallas_call(
        functools.partial(_ssd_fwd_kernel, nc=nc, di=di, gn=gn),
        grid=(b, nc),
        in_specs=[pl.BlockSpec((CHUNK, cd), fwd_g),
                  pl.BlockSpec((SUBLANES, cd), lambda bi, c: (jnp.maximum(fwd_g(bi, c)[0] * per8 - 1, 0), 0)),
                  pl.BlockSpec((SUBLANES, cd), lambda bi, c: (jnp.minimum((fwd_g(bi, c)[0] + 1) * per8, last8), 0)),
                  pl.BlockSpec((CHUNK, LANES), lambda bi, c: (fwd_g(bi, c)[0], di // LANES)),
                  _resident((3, cd)), _resident((1, cd)), _resident((1, LANES)), _resident((1, LANES))],
        out_specs=[pl.BlockSpec((CHUNK, di), fwd_l), pl.BlockSpec((CHUNK, di), fwd_l),
                   pl.BlockSpec((CHUNK, 2 * gn), fwd_l)],
        out_shape=[jax.ShapeDtypeStruct((b * s, di), F32), jax.ShapeDtypeStruct((b * s, di), F32),
                   jax.ShapeDtypeStruct((b * s, 2 * gn), BF16)],
        scratch_shapes=[state],
        compiler_params=_params(("parallel", "arbitrary")),
        name=f"ssd_fwd_s{s}",
    )(xbc, xbc, xbc, z_dt, conv_w, conv_b.reshape(1, cd), pad(dt_bias[0]), pad(a_log[0]))

    bwd_g = lambda bi, c: (cblk0 + bi * nc + nc - 1 - c, 0)
    bwd_l = lambda bi, c: (bi * nc + nc - 1 - c, 0)
    return pl.pallas_call(
        functools.partial(_ssd_bwd_kernel, di=di, gn=gn),
        grid=(b, nc),
        in_specs=[pl.BlockSpec((CHUNK, di), bwd_l), pl.BlockSpec((CHUNK, 2 * gn), bwd_l),
                  pl.BlockSpec((CHUNK, LANES), lambda bi, c: (bwd_g(bi, c)[0], di // LANES + 1)),
                  _resident((1, LANES)), _resident((1, LANES)),
                  pl.BlockSpec((CHUNK, di), bwd_l), pl.BlockSpec((CHUNK, di), bwd_g),
                  _resident((1, di)), _resident((1, di))] + shared.in_specs,
        out_specs=pl.BlockSpec((CHUNK, di), bwd_g),
        out_shape=jax.ShapeDtypeStruct((xbc.shape[0], di), BF16),
        input_output_aliases=shared.aliases(9),
        scratch_shapes=[state, pltpu.VMEM((CHUNK, di), F32)],
        compiler_params=_params(("parallel", "arbitrary")),
        name=f"ssd_bwd_s{s}",
    )(xs, bc, z_dt, pad(dt_bias[1]), pad(a_log[1]), y_fwd, z_dt, d_skip_row, norm_w.reshape(1, di), *shared.args)


def _derived_weights(p):
    heads = p["dt_bias"].shape[-1]
    di = p["ssm_norm"].shape[-1]
    cd = p["conv_w"].shape[-1]
    w_in = p["w_in"]
    depth, d, _ = w_in.shape
    o_z = 3 * ATT_W
    o_xbc = o_z + di
    o_dt = o_xbc + cd
    o_gate = o_dt + 2 * heads
    bf = lambda a: a.astype(BF16)

    def z_dt_cols(wl):
        pad = jnp.zeros((d, LANES - heads), F32)
        return jnp.concatenate([wl[:, o_z:o_xbc], wl[:, o_dt:o_dt + heads], pad,
                                wl[:, o_dt + heads:o_gate], pad], axis=-1)

    split = lambda cols: [bf(cols(w_in[l]))[None] for l in range(depth)]
    return dict(
        ffn1_w_gate=bf(p["ffn1_w_gate"]), ffn1_w_up=bf(p["ffn1_w_up"]), ffn1_w_down=bf(p["ffn1_w_down"]),
        ffn2_w_gate=bf(p["ffn2_w_gate"]), ffn2_w_up=bf(p["ffn2_w_up"]), ffn2_w_down=bf(p["ffn2_w_down"]),
        w_qkv=split(lambda wl: wl[:, :o_z]), w_xbc=split(lambda wl: wl[:, o_xbc:o_dt]),
        w_z_dt=split(z_dt_cols), w_gate=split(lambda wl: wl[:, o_gate:]),
        w_attn_out=bf(p["w_attn_out"]), w_ssm_out=bf(p["w_ssm_out"]), w_out=bf(p["w_out"]),
        d_skip_row=jnp.repeat(p["d_skip"], SSM_HEADDIM, axis=-1),
    )


def _layer(x_parts, xw, r, p, c, layer, sets, ropes):
    a = ffn_up(xw, r, c["ffn1_w_gate"], c["ffn1_w_up"], layer)
    out = None
    for row0, x_b in x_parts:
        out = ffn_down(a, c["ffn1_w_down"], layer, x_b, 0.5, p["mix_norm"][layer], rows=x_b.shape[0],
                       a_row0=row0, out_row0=row0, out_rows=a.shape[0], prev=out)
    x, xw, r = out
    qkv = matmul_scaled(xw, r, c["w_qkv"][layer], 0, tn=768, name="proj_qkv")
    z_dt = matmul_scaled(xw, r, c["w_z_dt"][layer], 0, tn=768, name="proj_z_dt")
    xbc = matmul_scaled(xw, r, c["w_xbc"][layer], 0, name="proj_xbc")
    gate_raw = matmul_scaled(xw, r, c["w_gate"][layer], 0, name="proj_gate")
    y_att = y_ssm = None
    for (row0, b, s), (cos, sin) in zip(sets, ropes):
        y_att = attention(qkv, cos, sin, p["q_norm"][layer], p["k_norm"][layer], row0, b, s, prev=y_att)
        y_ssm = ssd_mixer(xbc, z_dt, p["conv_w"][layer], p["conv_b"][layer], p["dt_bias"][layer],
                          p["a_log"][layer], c["d_skip_row"][layer:layer + 1], p["ssm_norm"][layer],
                          row0, b, s, prev=y_ssm)
    mix = gated_mix(y_att, y_ssm, c["w_attn_out"], c["w_ssm_out"], layer, gate_raw)
    x, xw, r = mix_out(mix, c["w_out"], layer, x, p["ffn2_norm"][layer])
    return x, ffn_up(xw, r, c["ffn2_w_gate"], c["ffn2_w_up"], layer)


def kernel(x_prompt, x_sample, ffn1_norm, ffn1_w_gate, ffn1_w_up, ffn1_w_down, mix_norm, w_in, q_norm, k_norm, conv_w, conv_b, dt_bias, a_log, d_skip, ssm_norm, w_attn_out, w_ssm_out, w_out, ffn2_norm, ffn2_w_gate, ffn2_w_up, ffn2_w_down):
    params = dict(ffn1_norm=ffn1_norm, ffn1_w_gate=ffn1_w_gate, ffn1_w_up=ffn1_w_up, ffn1_w_down=ffn1_w_down,
                  mix_norm=mix_norm, w_in=w_in, q_norm=q_norm, k_norm=k_norm, conv_w=conv_w, conv_b=conv_b,
                  dt_bias=dt_bias, a_log=a_log, d_skip=d_skip, ssm_norm=ssm_norm, w_attn_out=w_attn_out,
                  w_ssm_out=w_ssm_out, w_out=w_out, ffn2_norm=ffn2_norm, ffn2_w_gate=ffn2_w_gate,
                  ffn2_w_up=ffn2_w_up, ffn2_w_down=ffn2_w_down)
    d = x_prompt.shape[-1]
    bp, sp = x_prompt.shape[:2]
    bs, ss = x_sample.shape[:2]
    mp = bp * sp
    depth = ffn1_norm.shape[0]
    sets = ((0, bp, sp), (mp, bs, ss))
    ropes = (rope_tables(sp), rope_tables(ss))
    c = _derived_weights(params)
    m = mp + bs * ss
    x_parts = [(0, x_prompt.reshape(mp, d)), (mp, x_sample.reshape(bs * ss, d))]
    norm = None
    for row0, x_b in x_parts:
        norm = norm_prep(x_b, ffn1_norm[0], row0, m, prev=norm)
    xw, r = norm
    for layer in range(depth - 1):
        x, a = _layer(x_parts, xw, r, params, c, layer, sets, ropes)
        x, xw, r = ffn_down(a, c["ffn2_w_down"], layer, x, 0.5, ffn1_norm[layer + 1])
        x_parts = [(0, x)]
    x, a = _layer(x_parts, xw, r, params, c, depth - 1, sets, ropes)
    y_prompt = ffn_down(a, c["ffn2_w_down"], depth - 1, x, 0.5, rows=mp)
    y_sample = ffn_down(a, c["ffn2_w_down"], depth - 1, x, 0.5, rows=bs * ss, a_row0=mp, x_row0=mp)
    return y_prompt.reshape(x_prompt.shape), y_sample.reshape(x_sample.shape)
```

```python
import functools

import jax
import jax.numpy as jnp
from jax import lax
from jax.experimental import pallas as pl
from jax.experimental.pallas import tpu as pltpu

HEAD_DIM = 128
ATTN_GROUPS = ((128, 1), (512, 4), (2048, 16))
HEADS_PER_GROUP = 4
ATT_HEADS = HEADS_PER_GROUP * len(ATTN_GROUPS)
ATT_W = ATT_HEADS * HEAD_DIM
ATT_OUT = HEADS_PER_GROUP * HEAD_DIM
ATT_RADIUS = 64
ROPE_THETA = 10000.0
SSM_GROUPS = 4
SSM_HEADDIM = 64
CHUNK = 128
RMS_EPS = 1e-6
NEG_INF = -1e30

LANES = 128
SUBLANES = 8
V7X_VMEM_LIMIT = 56 * 1024 * 1024

F32 = jnp.float32
BF16 = jnp.bfloat16

assert all(w // (2 * d) == ATT_RADIUS for w, d in ATTN_GROUPS)


def _params(semantics):
    return pltpu.CompilerParams(dimension_semantics=semantics, vmem_limit_bytes=V7X_VMEM_LIMIT)


def _tile(n, pref):
    t = min(pref, n)
    t -= t % LANES
    while n % t:
        t -= LANES
    return t


def _resident(shape):
    return pl.BlockSpec(shape, lambda *_: (0,) * len(shape), pipeline_mode=pl.Buffered(1))


class _SharedOut:
    def __init__(self, prev):
        self.args = [] if prev is None else list(prev) if isinstance(prev, (tuple, list)) else [prev]
        self.in_specs = [pl.BlockSpec(memory_space=pl.ANY)] * len(self.args)

    def aliases(self, operand_index):
        return {operand_index + k: k for k in range(len(self.args))}


def _resident_layer(w, layer):
    return pl.BlockSpec((None,) + w.shape[1:], lambda *_: (layer, 0, 0), pipeline_mode=pl.Buffered(1))


def _row_rms(x):
    return lax.rsqrt(jnp.mean(x * x, axis=-1, keepdims=True) + RMS_EPS)


def _norm_prep_kernel(x_ref, w_ref, *rest):
    xw_ref, r_ref = rest[-2:]
    x = x_ref[...]
    xw_ref[...] = (x * w_ref[...]).astype(BF16)
    r_ref[...] = jnp.broadcast_to(_row_rms(x), r_ref.shape)


def norm_prep(x, w, out_row0, out_rows, prev=None, tm=256):
    m, d = x.shape
    shared = _SharedOut(prev)
    dst = lambda i: (i + out_row0 // tm, 0)
    return pl.pallas_call(
        _norm_prep_kernel,
        grid=(m // tm,),
        in_specs=[pl.BlockSpec((tm, d), lambda i: (i, 0)), _resident((1, d))] + shared.in_specs,
        out_specs=[pl.BlockSpec((tm, d), dst), pl.BlockSpec((tm, LANES), dst)],
        out_shape=[jax.ShapeDtypeStruct((out_rows, d), BF16), jax.ShapeDtypeStruct((out_rows, LANES), F32)],
        input_output_aliases=shared.aliases(2),
        compiler_params=_params(("parallel",)),
        name="norm_prep",
    )(x, w.reshape(1, d), *shared.args)


def _layer_cols(w, layer, tn):
    return pl.BlockSpec((None, w.shape[1], tn), lambda i, j: (layer, 0, j))


def _mm_kernel(a_ref, r_ref, w_ref, o_ref):
    y = jnp.dot(a_ref[...], w_ref[...], preferred_element_type=F32)
    o_ref[...] = (y * r_ref[:, 0:1]).astype(o_ref.dtype)


def matmul_scaled(a, r, w, layer, tm=1024, tn=1024, name="matmul"):
    m, k = a.shape
    n = w.shape[2]
    tm, tn = min(tm, m), _tile(n, tn)
    return pl.pallas_call(
        _mm_kernel,
        grid=(m // tm, n // tn),
        in_specs=[pl.BlockSpec((tm, k), lambda i, j: (i, 0)),
                  pl.BlockSpec((tm, LANES), lambda i, j: (i, 0)),
                  _layer_cols(w, layer, tn)],
        out_specs=pl.BlockSpec((tm, tn), lambda i, j: (i, j)),
        out_shape=jax.ShapeDtypeStruct((m, n), F32),
        compiler_params=_params(("parallel", "arbitrary")),
        name=name,
    )(a, r, w)


def _ffn_up_kernel(h_ref, r_ref, wg_ref, wu_ref, o_ref):
    h = h_ref[...]
    r = r_ref[:, 0:1]
    g = jnp.dot(h, wg_ref[...], preferred_element_type=F32) * r
    u = jnp.dot(h, wu_ref[...], preferred_element_type=F32) * r
    o_ref[...] = (g * jax.nn.sigmoid(g) * u).astype(o_ref.dtype)


def ffn_up(h, r, wg, wu, layer, tm=1024, tn=512):
    m, k = h.shape
    n = wg.shape[2]
    tm, tn = min(tm, m), _tile(n, tn)
    return pl.pallas_call(
        _ffn_up_kernel,
        grid=(m // tm, n // tn),
        in_specs=[pl.BlockSpec((tm, k), lambda i, j: (i, 0)),
                  pl.BlockSpec((tm, LANES), lambda i, j: (i, 0)),
                  _layer_cols(wg, layer, tn),
                  _layer_cols(wu, layer, tn)],
        out_specs=pl.BlockSpec((tm, tn), lambda i, j: (i, j)),
        out_shape=jax.ShapeDtypeStruct((m, n), BF16),
        compiler_params=_params(("parallel", "arbitrary")),
        name="ffn_up",
    )(h, r, wg, wu)


def _ffn_down_kernel(a_ref, w_ref, x_ref, *rest, scale, emit_norm):
    xn = x_ref[...] + scale * jnp.dot(a_ref[...], w_ref[...], preferred_element_type=F32)
    if emit_norm:
        wn_ref = rest[0]
        o_ref, xw_ref, r_ref = rest[-3:]
        xw_ref[...] = (xn * wn_ref[...]).astype(BF16)
        r_ref[...] = jnp.broadcast_to(_row_rms(xn), r_ref.shape)
    else:
        o_ref = rest[-1]
    o_ref[...] = xn


def ffn_down(a, w, layer, x, scale, next_norm_w=None, rows=None, a_row0=0, x_row0=0, out_row0=0, out_rows=None,
             prev=None, tm=256):
    k = a.shape[1]
    m = a.shape[0] if rows is None else rows
    out_rows = m if out_rows is None else out_rows
    n = w.shape[2]
    emit = next_norm_w is not None
    shared = _SharedOut(prev)
    at = lambda row0: (lambda i: (i + row0 // tm, 0))
    in_specs = [pl.BlockSpec((tm, k), at(a_row0)), _resident_layer(w, layer), pl.BlockSpec((tm, n), at(x_row0))]
    out_specs = [pl.BlockSpec((tm, n), at(out_row0))]
    out_shape = [jax.ShapeDtypeStruct((out_rows, n), F32)]
    args = [a, w, x]
    if emit:
        in_specs.append(_resident((1, n)))
        args.append(next_norm_w.reshape(1, n))
        out_specs += [pl.BlockSpec((tm, n), at(out_row0)), pl.BlockSpec((tm, LANES), at(out_row0))]
        out_shape += [jax.ShapeDtypeStruct((out_rows, n), BF16), jax.ShapeDtypeStruct((out_rows, LANES), F32)]
    out = pl.pallas_call(
        functools.partial(_ffn_down_kernel, scale=scale, emit_norm=emit),
        grid=(m // tm,),
        in_specs=in_specs + shared.in_specs,
        out_specs=out_specs,
        out_shape=out_shape,
        input_output_aliases=shared.aliases(len(args)),
        compiler_params=_params(("parallel",)),
        name="ffn_down",
    )(*args, *shared.args)
    return out if emit else out[0]


def _mix_out_kernel(a_ref, w_ref, x_ref, wn_ref, o_ref, xw_ref, r_ref, ssq_ref, *, d):
    j = pl.program_id(1)
    xn = x_ref[...] + jnp.dot(a_ref[...], w_ref[...], preferred_element_type=F32)
    o_ref[...] = xn
    xw_ref[...] = (xn * wn_ref[...]).astype(BF16)
    part = jnp.broadcast_to(jnp.sum(xn * xn, axis=-1, keepdims=True), ssq_ref.shape)

    @pl.when(j == 0)
    def _():
        ssq_ref[...] = part

    @pl.when(j > 0)
    def _():
        ssq_ref[...] += part

    @pl.when(j == pl.num_programs(1) - 1)
    def _():
        r_ref[...] = lax.rsqrt(ssq_ref[...] / d + RMS_EPS)


def mix_out(a, w, layer, x, next_norm_w, tm=1024, tn=512):
    m, k = a.shape
    n = w.shape[2]
    tm, tn = min(tm, m), _tile(n, tn)
    return pl.pallas_call(
        functools.partial(_mix_out_kernel, d=n),
        grid=(m // tm, n // tn),
        in_specs=[pl.BlockSpec((tm, k), lambda i, j: (i, 0)),
                  _layer_cols(w, layer, tn),
                  pl.BlockSpec((tm, tn), lambda i, j: (i, j)),
                  pl.BlockSpec((1, tn), lambda i, j: (0, j))],
        out_specs=[pl.BlockSpec((tm, tn), lambda i, j: (i, j)),
                   pl.BlockSpec((tm, tn), lambda i, j: (i, j)),
                   pl.BlockSpec((tm, LANES), lambda i, j: (i, 0))],
        out_shape=[jax.ShapeDtypeStruct((m, n), F32), jax.ShapeDtypeStruct((m, n), BF16),
                   jax.ShapeDtypeStruct((m, LANES), F32)],
        scratch_shapes=[pltpu.VMEM((tm, LANES), F32)],
        compiler_params=_params(("parallel", "arbitrary")),
        name="mix_out",
    )(a, w, x, next_norm_w.reshape(1, n))


def _gated_mix_kernel(ya_ref, ys_ref, wa_ref, ws_ref, ga_ref, gs_ref, o_ref):
    pa = jnp.dot(ya_ref[...], wa_ref[...], preferred_element_type=F32)
    ps = jnp.dot(ys_ref[...], ws_ref[...], preferred_element_type=F32)
    o_ref[...] = (jax.nn.sigmoid(ga_ref[...]) * pa + jax.nn.sigmoid(gs_ref[...]) * ps).astype(o_ref.dtype)


def gated_mix(ya, ys, wa, ws, layer, gate_raw, tm=256):
    m, ka = ya.shape
    ks = ys.shape[1]
    n = wa.shape[2]
    row = lambda i: (i, 0)
    return pl.pallas_call(
        _gated_mix_kernel,
        grid=(m // tm,),
        in_specs=[pl.BlockSpec((tm, ka), row), pl.BlockSpec((tm, ks), row),
                  _resident_layer(wa, layer), _resident_layer(ws, layer),
                  pl.BlockSpec((tm, n), row), pl.BlockSpec((tm, n), lambda i: (i, 1))],
        out_specs=pl.BlockSpec((tm, n), row),
        out_shape=jax.ShapeDtypeStruct((m, n), BF16),
        compiler_params=_params(("parallel",)),
        name="gated_mix",
    )(ya, ys, wa, ws, gate_raw, gate_raw)


ATT_TQ = 128
ATT_TK = ATT_TQ + 2 * ATT_RADIUS
ATT_UNROLL = 8
ATT_SCORE_UNROLL = 16


def _attn_group(q_ref, k_ref, v_ref, cos_ref, sin_ref, qw_ref, kw_ref,
                qd_ref, kd_ref, vd_ref, bias_ref, acc_ref, m_ref, l_ref, *, s, dil, first):
    n = s // dil
    nblk = n // ATT_TQ
    seg = n + 2 * ATT_RADIUS
    scale = HEAD_DIM ** -0.5

    def rows_of(c):
        r, mb = c // nblk, c % nblk
        if dil == 1:
            return r, mb, pl.ds(pl.multiple_of(c * ATT_TQ, ATT_TQ), ATT_TQ)
        return r, mb, pl.ds(r + mb * (ATT_TQ * dil), ATT_TQ, stride=dil)

    def norm_rope(x, w_ref, c, out_scale):
        rows = pl.ds(pl.multiple_of(c * ATT_TQ, ATT_TQ), ATT_TQ)
        w, w_rolled = w_ref[0:1, :], w_ref[1:2, :]
        xr = pltpu.roll(x, HEAD_DIM // 2, 1)
        y = x * (cos_ref[rows, :] * w) + xr * (sin_ref[rows, :] * w_rolled)
        sq = x * x
        hi = sq.astype(BF16)
        lo = (sq - hi.astype(F32)).astype(BF16)
        ssq = jnp.dot(jnp.concatenate([hi, lo], axis=1), sum_lanes, preferred_element_type=F32)
        return y * (lax.rsqrt(ssq * (1.0 / HEAD_DIM) + RMS_EPS) * out_scale)

    sum_lanes = jnp.ones((2 * HEAD_DIM, HEAD_DIM), BF16)
    zeros = jnp.zeros((ATT_RADIUS, 2 * HEAD_DIM), BF16)
    ones = jnp.ones((ATT_TQ, HEAD_DIM), BF16)

    def zero_pads(r, carry):
        lo = pl.multiple_of(r * seg, ATT_RADIUS)
        hi = pl.multiple_of(r * seg + ATT_RADIUS + n, ATT_RADIUS)
        kd_ref[pl.ds(lo, ATT_RADIUS), :] = zeros[:, :HEAD_DIM]
        kd_ref[pl.ds(hi, ATT_RADIUS), :] = zeros[:, :HEAD_DIM]
        vd_ref[pl.ds(lo, ATT_RADIUS), :] = zeros
        vd_ref[pl.ds(hi, ATT_RADIUS), :] = zeros
        return carry

    lax.fori_loop(0, dil, zero_pads, 0)

    def prep(c2, carry):
        for u in range(ATT_UNROLL):
            c = c2 * ATT_UNROLL + u
            r, mb, rows = rows_of(c)
            qn = norm_rope(q_ref[rows, :], qw_ref, c, scale)
            kn = norm_rope(k_ref[rows, :], kw_ref, c, 1.0)
            qd_ref[pl.ds(pl.multiple_of(c * ATT_TQ, ATT_TQ), ATT_TQ), :] = qn.astype(BF16)
            dst = pl.ds(pl.multiple_of(r * seg + ATT_RADIUS + mb * ATT_TQ, ATT_RADIUS), ATT_TQ)
            kd_ref[dst, :] = kn.astype(BF16)
            vd_ref[dst, :HEAD_DIM] = v_ref[rows, :].astype(BF16)
            vd_ref[dst, HEAD_DIM:] = ones
        return carry

    lax.fori_loop(0, s // (ATT_TQ * ATT_UNROLL), prep, 0)

    def block(c2, carry):
        parts = []
        for u in range(ATT_SCORE_UNROLL):
            c = c2 * ATT_SCORE_UNROLL + u
            r, mb, rows = rows_of(c)
            qn = qd_ref[pl.ds(pl.multiple_of(c * ATT_TQ, ATT_TQ), ATT_TQ), :]
            win = pl.ds(pl.multiple_of(r * seg + mb * ATT_TQ, ATT_RADIUS), ATT_TK)
            edge = jnp.where(mb == 0, 1, 0) + jnp.where(mb == nblk - 1, 2, 0)
            sc = lax.dot_general(qn, kd_ref[win, :], (((1,), (1,)), ((), ())), preferred_element_type=F32)
            sc = sc + bias_ref[edge]
            mx = jnp.max(sc, axis=-1, keepdims=True)
            p = jnp.exp(sc - mx).astype(BF16)
            pv = jnp.dot(p, vd_ref[win, :], preferred_element_type=F32)
            parts.append((rows, mx, pv))
        for rows, mx, pv in parts:
            m_b = jnp.broadcast_to(mx, (ATT_TQ, HEAD_DIM))
            pv, l_b = pv[:, :HEAD_DIM], pv[:, HEAD_DIM:]
            if first:
                acc_n, m_n, l_n = pv, m_b, l_b
            else:
                m_o = m_ref[rows, :]
                m_n = jnp.maximum(m_o, m_b)
                a, b = jnp.exp(m_o - m_n), jnp.exp(m_b - m_n)
                acc_n = a * acc_ref[rows, :] + b * pv
                l_n = a * l_ref[rows, :] + b * l_b
            acc_ref[rows, :] = acc_n
            m_ref[rows, :] = m_n
            l_ref[rows, :] = l_n
        return carry

    lax.fori_loop(0, s // (ATT_TQ * ATT_SCORE_UNROLL), block, 0)


def _attn_kernel(q_ref, k_ref, v_ref, cos_ref, sin_ref, qw_ref, kw_ref, *rest, s):
    o_ref, qd_ref, kd_ref, vd_ref, bias_ref, acc_ref, m_ref, l_ref = rest[-8:]
    g = pl.program_id(2)
    ng = len(ATTN_GROUPS)

    qi = lax.broadcasted_iota(jnp.int32, (ATT_TQ, ATT_TK), 0)
    kj = lax.broadcasted_iota(jnp.int32, (ATT_TQ, ATT_TK), 1)
    band = jnp.abs(kj - ATT_RADIUS - qi) <= ATT_RADIUS
    for e in range(4):
        ok = band
        if e & 1:
            ok = ok & (kj >= ATT_RADIUS)
        if e & 2:
            ok = ok & (kj < ATT_TQ + ATT_RADIUS)
        bias_ref[e] = jnp.where(ok, 0.0, NEG_INF)

    for step, (_, dil) in enumerate(reversed(ATTN_GROUPS)):
        @pl.when(g == step)
        def _(dil=dil, step=step):
            _attn_group(q_ref, k_ref, v_ref, cos_ref, sin_ref, qw_ref, kw_ref,
                        qd_ref, kd_ref, vd_ref, bias_ref, acc_ref, m_ref, l_ref,
                        s=s, dil=dil, first=step == 0)

    @pl.when(g == ng - 1)
    def _():
        rows_per = 2 * ATT_TQ

        def finish(c, carry):
            rows = pl.ds(pl.multiple_of(c * rows_per, rows_per), rows_per)
            o_ref[rows, :] = (acc_ref[rows, :] / l_ref[rows, :]).astype(o_ref.dtype)
            return carry

        lax.fori_loop(0, s // rows_per, finish, 0)


def attention(qkv, cos, sin, q_norm, k_norm, row0, b, s, prev=None):
    shared = _SharedOut(prev)
    blk0 = row0 // s
    hpg = HEADS_PER_GROUP
    max_dil = max(d for _, d in ATTN_GROUPS)
    kv_rows = s + 2 * ATT_RADIUS * max_dil

    def col(base):
        return lambda bi, j, g: (blk0 + bi, base + (len(ATTN_GROUPS) - 1 - g) * hpg + j)

    def with_rolled(w):
        return jnp.stack([w, jnp.roll(w, HEAD_DIM // 2)])

    return pl.pallas_call(
        functools.partial(_attn_kernel, s=s),
        grid=(b, hpg, len(ATTN_GROUPS)),
        in_specs=[pl.BlockSpec((s, HEAD_DIM), col(0)),
                  pl.BlockSpec((s, HEAD_DIM), col(ATT_HEADS)),
                  pl.BlockSpec((s, HEAD_DIM), col(2 * ATT_HEADS)),
                  pl.BlockSpec((None, s, HEAD_DIM), lambda bi, j, g: (len(ATTN_GROUPS) - 1 - g, 0, 0)),
                  pl.BlockSpec((None, s, HEAD_DIM), lambda bi, j, g: (len(ATTN_GROUPS) - 1 - g, 0, 0)),
                  _resident((2, HEAD_DIM)),
                  _resident((2, HEAD_DIM))] + shared.in_specs,
        out_specs=pl.BlockSpec((s, HEAD_DIM), lambda bi, j, g: (blk0 + bi, j)),
        out_shape=jax.ShapeDtypeStruct((qkv.shape[0], ATT_OUT), BF16),
        input_output_aliases=shared.aliases(7),
        scratch_shapes=[pltpu.VMEM((s, HEAD_DIM), BF16),
                        pltpu.VMEM((kv_rows, HEAD_DIM), BF16),
                        pltpu.VMEM((kv_rows, 2 * HEAD_DIM), BF16),
                        pltpu.VMEM((4, ATT_TQ, ATT_TK), F32),
                        pltpu.VMEM((s, HEAD_DIM), F32),
                        pltpu.VMEM((s, HEAD_DIM), F32),
                        pltpu.VMEM((s, HEAD_DIM), F32)],
        compiler_params=_params(("parallel", "parallel", "arbitrary")),
        name=f"dilated_attention_s{s}",
    )(qkv, qkv, qkv, cos, sin, with_rolled(q_norm), with_rolled(k_norm), *shared.args)


def rope_tables(s):
    inv_freq = ROPE_THETA ** (-jnp.arange(0, HEAD_DIM, 2, dtype=F32) / HEAD_DIM)
    ang = jnp.arange(s, dtype=F32)[:, None] * inv_freq[None, :]
    cos, sin = jnp.cos(ang), jnp.sin(ang)
    cos, sin = jnp.concatenate([cos, cos], axis=-1), jnp.concatenate([-sin, sin], axis=-1)

    def by_subsequence(t):
        return jnp.stack([t.reshape(s // d, d, HEAD_DIM).swapaxes(0, 1).reshape(s, HEAD_DIM)
                          for _, d in ATTN_GROUPS])

    return by_subsequence(cos), by_subsequence(sin)


def _scan_rows(x, reverse):
    row = lax.broadcasted_iota(jnp.int32, x.shape, 0)
    sh = 1
    while sh < CHUNK:
        if reverse:
            x = x + jnp.where(row < CHUNK - sh, pltpu.roll(x, CHUNK - sh, 0), 0.0)
        else:
            x = x + jnp.where(row >= sh, pltpu.roll(x, sh, 0), 0.0)
        sh *= 2
    return x


def _ssd_fwd_kernel(xbc_ref, xprev_ref, xnext_ref, dt_ref, cw_ref, cb_ref, dtb_ref, alog_ref,
                    y_ref, xs_ref, bc_ref, state_ref, *, nc, di, gn):
    chunk = pl.program_id(1)
    x = xbc_ref[...]
    row = lax.broadcasted_iota(jnp.int32, (SUBLANES, 1), 0)
    prev_row = jnp.where(chunk > 0, xprev_ref[SUBLANES - 1:SUBLANES, :], 0.0)
    next_row = jnp.where(chunk < nc - 1, xnext_ref[0:1, :], 0.0)
    xm1 = pltpu.roll(x, 1, 0)
    xm1 = jnp.concatenate([jnp.where(row == 0, prev_row, xm1[:SUBLANES]), xm1[SUBLANES:]], axis=0)
    xp1 = pltpu.roll(x, CHUNK - 1, 0)
    xp1 = jnp.concatenate([xp1[:-SUBLANES], jnp.where(row == SUBLANES - 1, next_row, xp1[-SUBLANES:])], axis=0)
    xc = xm1 * cw_ref[0:1, :] + x * cw_ref[1:2, :] + xp1 * cw_ref[2:3, :] + cb_ref[...]
    xc = xc * jax.nn.sigmoid(xc)
    xs_ref[...] = xc[:, :di]
    bc_ref[...] = xc[:, di:].astype(BF16)
    _ssd_chunk(xs_ref, bc_ref, dt_ref, dtb_ref, alog_ref, y_ref, state_ref, di=di, gn=gn, reverse=False)


def _ssd_bwd_kernel(xs_ref, bc_ref, dt_ref, dtb_ref, alog_ref, yf_ref, z_ref, dskip_ref, nw_ref,
                    *rest, di, gn):
    o_ref, state_ref, y_ref = rest[-3:]
    _ssd_chunk(xs_ref, bc_ref, dt_ref, dtb_ref, alog_ref, y_ref, state_ref, di=di, gn=gn, reverse=True)
    y = yf_ref[...] + y_ref[...] + dskip_ref[...] * xs_ref[...]
    z = z_ref[...]
    y = y * (z * jax.nn.sigmoid(z))
    o_ref[...] = ((y * _row_rms(y)) * nw_ref[...]).astype(o_ref.dtype)


def _ssd_chunk(xs_ref, bc_ref, dt_ref, dtb_ref, alog_ref, y_ref, state_ref, *, di, gn, reverse):
    gw = di // SSM_GROUPS
    n_state = gn // SSM_GROUPS
    half = SSM_HEADDIM

    @pl.when(pl.program_id(1) == 0)
    def _():
        state_ref[...] = jnp.zeros_like(state_ref)

    dtr = dt_ref[...] + dtb_ref[...]
    dt = jnp.maximum(dtr, 0.0) + jnp.log1p(jnp.exp(-jnp.abs(dtr)))
    acum = _scan_rows(dt * (-jnp.exp(alog_ref[...])), reverse)
    total = acum[0:1, :] if reverse else acum[CHUNK - 1:CHUNK, :]
    wend = jnp.exp(total - acum) * dt
    src_t = (acum - jnp.log(dt)).T

    li = lax.broadcasted_iota(jnp.int32, (CHUNK, CHUNK), 0)
    si = lax.broadcasted_iota(jnp.int32, (CHUNK, CHUNK), 1)
    causal = (si >= li) if reverse else (li >= si)
    lane = lax.broadcasted_iota(jnp.int32, (CHUNK, LANES), 1)
    lo_half = lane < half
    lo_mask = lo_half.astype(BF16)
    hi_mask = 1 - lo_mask

    def lanes_of(col_vals, h):
        return jnp.broadcast_to(col_vals[:, h:h + 1], (CHUNK, LANES))

    for g in range(SSM_GROUPS):
        b_g = bc_ref[:, g * n_state:(g + 1) * n_state]
        c_g = bc_ref[:, gn + g * n_state:gn + (g + 1) * n_state]
        cb = lax.dot_general(c_g, b_g, (((1,), (1,)), ((), ())), preferred_element_type=F32).astype(BF16)
        st = state_ref[g]
        y_off = jnp.dot(c_g, st.astype(BF16), preferred_element_type=F32)
        wx_parts, dec_parts = [], []
        for pr in range(gw // LANES):
            col0 = g * gw + pr * LANES
            h0 = col0 // half
            xp = xs_ref[:, col0:col0 + LANES]
            sc, ecol, wcol = [], [], []
            for h in (h0, h0 + 1):
                a_col = lanes_of(acum, h)
                dec = jnp.exp(jnp.where(causal, a_col - src_t[h:h + 1, :], -jnp.inf))
                sc.append(cb * dec.astype(BF16))
                ecol.append(jnp.exp(a_col))
                wcol.append(lanes_of(wend, h))
            s2 = jnp.concatenate(sc, axis=1)
            xp16 = xp.astype(BF16)
            x2 = jnp.concatenate([xp16 * lo_mask, xp16 * hi_mask], axis=0)
            y = jnp.dot(s2, x2, preferred_element_type=F32)
            y = y + y_off[:, pr * LANES:(pr + 1) * LANES] * jnp.where(lo_half, ecol[0], ecol[1])
            y_ref[:, col0:col0 + LANES] = y
            wx_parts.append((jnp.where(lo_half, wcol[0], wcol[1]) * xp).astype(BF16))
            tot0 = jnp.broadcast_to(total[:, h0:h0 + 1], (1, LANES))
            tot1 = jnp.broadcast_to(total[:, h0 + 1:h0 + 2], (1, LANES))
            dec_parts.append(jnp.exp(jnp.where(lo_half[0:1, :], tot0, tot1)))
        wx = jnp.concatenate(wx_parts, axis=1) if len(wx_parts) > 1 else wx_parts[0]
        sdec = jnp.concatenate(dec_parts, axis=1) if len(dec_parts) > 1 else dec_parts[0]
        upd = lax.dot_general(b_g, wx, (((0,), (0,)), ((), ())), preferred_element_type=F32)
        state_ref[g] = st * sdec + upd


def ssd_mixer(xbc, z_dt, conv_w, conv_b, dt_bias, a_log, d_skip_row, norm_w, row0, b, s, prev=None):
    shared = _SharedOut(prev)
    cd = conv_w.shape[1]
    heads = dt_bias.shape[-1]
    di = heads * SSM_HEADDIM
    gn = (cd - di) // 2
    nc = s // CHUNK
    cblk0 = row0 // CHUNK
    per8 = CHUNK // SUBLANES
    last8 = xbc.shape[0] // SUBLANES - 1
    pad = lambda v: jnp.zeros((1, LANES), F32).at[0, :heads].set(v)
    state = pltpu.VMEM((SSM_GROUPS, gn // SSM_GROUPS, di // SSM_GROUPS), F32)

    fwd_g = lambda bi, c: (cblk0 + bi * nc + c, 0)
    fwd_l = lambda bi, c: (bi * nc + c, 0)
    y_fwd, xs, bc = pl.pallas_call(
        functools.partial(_ssd_fwd_kernel, nc=nc, di=di, gn=gn),
        grid=(b, nc),
        in_specs=[pl.BlockSpec((CHUNK, cd), fwd_g),
                  pl.BlockSpec((SUBLANES, cd), lambda bi, c: (jnp.maximum(fwd_g(bi, c)[0] * per8 - 1, 0), 0)),
                  pl.BlockSpec((SUBLANES, cd), lambda bi, c: (jnp.minimum((fwd_g(bi, c)[0] + 1) * per8, last8), 0)),
                  pl.BlockSpec((CHUNK, LANES), lambda bi, c: (fwd_g(bi, c)[0], di // LANES)),
                  _resident((3, cd)), _resident((1, cd)), _resident((1, LANES)), _resident((1, LANES))],
        out_specs=[pl.BlockSpec((CHUNK, di), fwd_l), pl.BlockSpec((CHUNK, di), fwd_l),
                   pl.BlockSpec((CHUNK, 2 * gn), fwd_l)],
        out_shape=[jax.ShapeDtypeStruct((b * s, di), F32), jax.ShapeDtypeStruct((b * s, di), F32),
                   jax.ShapeDtypeStruct((b * s, 2 * gn), BF16)],
        scratch_shapes=[state],
        compiler_params=_params(("parallel", "arbitrary")),
        name=f"ssd_fwd_s{s}",
    )(xbc, xbc, xbc, z_dt, conv_w, conv_b.reshape(1, cd), pad(dt_bias[0]), pad(a_log[0]))

    bwd_g = lambda bi, c: (cblk0 + bi * nc + nc - 1 - c, 0)
    bwd_l = lambda bi, c: (bi * nc + nc - 1 - c, 0)
    return pl.pallas_call(
        functools.partial(_ssd_bwd_kernel, di=di, gn=gn),
        grid=(b, nc),
        in_specs=[pl.BlockSpec((CHUNK, di), bwd_l), pl.BlockSpec((CHUNK, 2 * gn), bwd_l),
                  pl.BlockSpec((CHUNK, LANES), lambda bi, c: (bwd_g(bi, c)[0], di // LANES + 1)),
                  _resident((1, LANES)), _resident((1, LANES)),
                  pl.BlockSpec((CHUNK, di), bwd_l), pl.BlockSpec((CHUNK, di), bwd_g),
                  _resident((1, di)), _resident((1, di))] + shared.in_specs,
        out_specs=pl.BlockSpec((CHUNK, di), bwd_g),
        out_shape=jax.ShapeDtypeStruct((xbc.shape[0], di), BF16),
        input_output_aliases=shared.aliases(9),
        scratch_shapes=[state, pltpu.VMEM((CHUNK, di), F32)],
        compiler_params=_params(("parallel", "arbitrary")),
        name=f"ssd_bwd_s{s}",
    )(xs, bc, z_dt, pad(dt_bias[1]), pad(a_log[1]), y_fwd, z_dt, d_skip_row, norm_w.reshape(1, di), *shared.args)


def _derived_weights(p):
    heads = p["dt_bias"].shape[-1]
    di = p["ssm_norm"].shape[-1]
    cd = p["conv_w"].shape[-1]
    w_in = p["w_in"]
    depth, d, _ = w_in.shape
    o_z = 3 * ATT_W
    o_xbc = o_z + di
    o_dt = o_xbc + cd
    o_gate = o_dt + 2 * heads
    bf = lambda a: a.astype(BF16)

    def z_dt_cols(wl):
        pad = jnp.zeros((d, LANES - heads), F32)
        return jnp.concatenate([wl[:, o_z:o_xbc], wl[:, o_dt:o_dt + heads], pad,
                                wl[:, o_dt + heads:o_gate], pad], axis=-1)

    split = lambda cols: [bf(cols(w_in[l]))[None] for l in range(depth)]
    return dict(
        ffn1_w_gate=bf(p["ffn1_w_gate"]), ffn1_w_up=bf(p["ffn1_w_up"]), ffn1_w_down=bf(p["ffn1_w_down"]),
        ffn2_w_gate=bf(p["ffn2_w_gate"]), ffn2_w_up=bf(p["ffn2_w_up"]), ffn2_w_down=bf(p["ffn2_w_down"]),
        w_qkv=split(lambda wl: wl[:, :o_z]), w_xbc=split(lambda wl: wl[:, o_xbc:o_dt]),
        w_z_dt=split(z_dt_cols), w_gate=split(lambda wl: wl[:, o_gate:]),
        w_attn_out=bf(p["w_attn_out"]), w_ssm_out=bf(p["w_ssm_out"]), w_out=bf(p["w_out"]),
        d_skip_row=jnp.repeat(p["d_skip"], SSM_HEADDIM, axis=-1),
    )


def _layer(x_parts, xw, r, p, c, layer, sets, ropes):
    a = ffn_up(xw, r, c["ffn1_w_gate"], c["ffn1_w_up"], layer)
    out = None
    for row0, x_b in x_parts:
        out = ffn_down(a, c["ffn1_w_down"], layer, x_b, 0.5, p["mix_norm"][layer], rows=x_b.shape[0],
                       a_row0=row0, out_row0=row0, out_rows=a.shape[0], prev=out)
    x, xw, r = out
    qkv = matmul_scaled(xw, r, c["w_qkv"][layer], 0, tn=768, name="proj_qkv")
    z_dt = matmul_scaled(xw, r, c["w_z_dt"][layer], 0, tn=768, name="proj_z_dt")
    xbc = matmul_scaled(xw, r, c["w_xbc"][layer], 0, name="proj_xbc")
    gate_raw = matmul_scaled(xw, r, c["w_gate"][layer], 0, name="proj_gate")
    y_att = y_ssm = None
    for (row0, b, s), (cos, sin) in zip(sets, ropes):
        y_att = attention(qkv, cos, sin, p["q_norm"][layer], p["k_norm"][layer], row0, b, s, prev=y_att)
        y_ssm = ssd_mixer(xbc, z_dt, p["conv_w"][layer], p["conv_b"][layer], p["dt_bias"][layer],
                          p["a_log"][layer], c["d_skip_row"][layer:layer + 1], p["ssm_norm"][layer],
                          row0, b, s, prev=y_ssm)
    mix = gated_mix(y_att, y_ssm, c["w_attn_out"], c["w_ssm_out"], layer, gate_raw)
    x, xw, r = mix_out(mix, c["w_out"], layer, x, p["ffn2_norm"][layer])
    return x, ffn_up(xw, r, c["ffn2_w_gate"], c["ffn2_w_up"], layer)


def kernel(x_prompt, x_sample, ffn1_norm, ffn1_w_gate, ffn1_w_up, ffn1_w_down, mix_norm, w_in, q_norm, k_norm, conv_w, conv_b, dt_bias, a_log, d_skip, ssm_norm, w_attn_out, w_ssm_out, w_out, ffn2_norm, ffn2_w_gate, ffn2_w_up, ffn2_w_down):
    params = dict(ffn1_norm=ffn1_norm, ffn1_w_gate=ffn1_w_gate, ffn1_w_up=ffn1_w_up, ffn1_w_down=ffn1_w_down,
                  mix_norm=mix_norm, w_in=w_in, q_norm=q_norm, k_norm=k_norm, conv_w=conv_w, conv_b=conv_b,
                  dt_bias=dt_bias, a_log=a_log, d_skip=d_skip, ssm_norm=ssm_norm, w_attn_out=w_attn_out,
                  w_ssm_out=w_ssm_out, w_out=w_out, ffn2_norm=ffn2_norm, ffn2_w_gate=ffn2_w_gate,
                  ffn2_w_up=ffn2_w_up, ffn2_w_down=ffn2_w_down)
    d = x_prompt.shape[-1]
    bp, sp = x_prompt.shape[:2]
    bs, ss = x_sample.shape[:2]
    mp = bp * sp
    depth = ffn1_norm.shape[0]
    sets = ((0, bp, sp), (mp, bs, ss))
    ropes = (rope_tables(sp), rope_tables(ss))
    c = _derived_weights(params)
    m = mp + bs * ss
    x_parts = [(0, x_prompt.reshape(mp, d)), (mp, x_sample.reshape(bs * ss, d))]
    norm = None
    for row0, x_b in x_parts:
        norm = norm_prep(x_b, ffn1_norm[0], row0, m, prev=norm)
    xw, r = norm
    for layer in range(depth - 1):
        x, a = _layer(x_parts, xw, r, params, c, layer, sets, ropes)
        x, xw, r = ffn_down(a, c["ffn2_w_down"], layer, x, 0.5, ffn1_norm[layer + 1])
        x_parts = [(0, x)]
    x, a = _layer(x_parts, xw, r, params, c, depth - 1, sets, ropes)
    y_prompt = ffn_down(a, c["ffn2_w_down"], depth - 1, x, 0.5, rows=mp)
    y_sample = ffn_down(a, c["ffn2_w_down"], depth - 1, x, 0.5, rows=bs * ss, a_row0=mp, x_row0=mp)
    return y_prompt.reshape(x_prompt.shape), y_sample.reshape(x_sample.shape)
```
